```python
import math
import jax, jax.numpy as jnp
from jax import lax
import numpy as np

D_MODEL = 4096
BATCH = 4
SEQ = 2048
DEPTH = 2
DEC_BATCH = 8
DEC_SEQ = 1
PAST_LEN = 16384
PAGE_SIZE = 128

HEAD_DIM = 128
FOX_HEADS = D_MODEL // (2 * HEAD_DIM)
FOX_WIDTH = FOX_HEADS * HEAD_DIM
DIFF_HEADS = D_MODEL // (4 * HEAD_DIM)
DIFF_WIDTH = DIFF_HEADS * 2 * HEAD_DIM
N_IN = 4 * FOX_WIDTH + FOX_HEADS + 4 * DIFF_WIDTH + 2 * D_MODEL
N_META = 16
Q_BLOCK = 128
ROPE_THETA = 10000.0
ALPHA = (2 * DEPTH) ** 0.25
BETA = (8 * DEPTH) ** -0.25
EPS = 1e-5
FORGET_BIAS_INIT = 2.0

kernel_name = 'hybrid_fox_diff_decoder_step'


def layer_norm(x, g, b):
    xf = x.astype(jnp.float32)
    mu = jnp.mean(xf, axis=-1, keepdims=True)
    var = jnp.mean(jnp.square(xf - mu), axis=-1, keepdims=True)
    return ((xf - mu) * lax.rsqrt(var + EPS) * g + b).astype(x.dtype)


def head_rms_norm(x, g):
    xf = x.astype(jnp.float32)
    return (xf * lax.rsqrt(jnp.mean(jnp.square(xf), axis=-1, keepdims=True) + EPS) * g).astype(x.dtype)


def rope(x, pos):
    half = HEAD_DIM // 2
    inv = 1.0 / (ROPE_THETA ** (jnp.arange(half, dtype=jnp.float32) * 2.0 / HEAD_DIM))
    ang = pos.astype(jnp.float32)[:, None] * inv[None, :]
    cos = jnp.cos(ang)[None, :, None, None, :]
    sin = jnp.sin(ang)[None, :, None, None, :]
    xf = x.astype(jnp.float32)
    x1, x2 = xf[..., :half], xf[..., half:]
    return jnp.concatenate([x1 * cos - x2 * sin, x2 * cos + x1 * sin], axis=-1).astype(x.dtype)


def lambda_init(layer):
    return 0.8 - 0.6 * math.exp(-0.3 * layer)


def diff_lambda(lq1, lk1, lq2, lk2, lam_init):
    e1 = jnp.exp(jnp.sum(lq1.astype(jnp.float32) * lk1.astype(jnp.float32)))
    e2 = jnp.exp(jnp.sum(lq2.astype(jnp.float32) * lk2.astype(jnp.float32)))
    return e1 - e2 + lam_init


def in_project(h, w_in, b_f, pos):
    b, t, _ = h.shape
    sizes = (FOX_WIDTH,) * 4 + (FOX_HEADS,) + (DIFF_WIDTH,) * 4 + (D_MODEL, D_MODEL)
    idx = np.cumsum(sizes)[:-1].tolist()
    qa, ka, va, za, fa, qb, kb, vb, zb, ga, gb = jnp.split(h @ w_in, idx, axis=-1)
    qa = qa.reshape(b, t, FOX_HEADS, HEAD_DIM)
    ka = ka.reshape(b, t, FOX_HEADS, HEAD_DIM)
    va = va.reshape(b, t, FOX_HEADS, HEAD_DIM)
    logf = jax.nn.log_sigmoid((fa + b_f).astype(jnp.float32))
    qb = rope(qb.reshape(b, t, DIFF_HEADS, 2, HEAD_DIM), pos)
    kb = rope(kb.reshape(b, t, DIFF_HEADS, 2, HEAD_DIM), pos)
    vb = vb.reshape(b, t, DIFF_HEADS, 2 * HEAD_DIM)
    return qa, ka, va, za, logf, qb, kb, vb, zb, ga, gb


def fox_attend(q, cq, qpos, k, v, ck, kpos):
    s = jnp.einsum('bqhd,bkhd->bhqk', q, k, preferred_element_type=jnp.float32) * (HEAD_DIM ** -0.5)
    s = s + (jnp.transpose(cq, (0, 2, 1))[..., :, None] - jnp.transpose(ck, (0, 2, 1))[..., None, :])
    mask = kpos[None, :] <= qpos[:, None]
    s = jnp.where(mask[None, None], s, -jnp.inf)
    p = jax.nn.softmax(s, axis=-1)
    return jnp.einsum('bhqk,bkhd->bqhd', p.astype(v.dtype), v)


def diff_attend(q, qpos, k, v, kpos, lam):
    s = jnp.einsum('bqhcd,bkhcd->bhcqk', q, k, preferred_element_type=jnp.float32) * (HEAD_DIM ** -0.5)
    mask = kpos[None, :] <= qpos[:, None]
    s = jnp.where(mask[None, None, None], s, -jnp.inf)
    p = jax.nn.softmax(s, axis=-1)
    a = p[:, :, 0] - lam * p[:, :, 1]
    return jnp.einsum('bhqk,bkhe->bqhe', a.astype(v.dtype), v)


def block_sweep(attend, q_parts, qpos):
    head = attend(tuple(a[:, :N_META] for a in q_parts), qpos[:N_META])
    n_blk = (qpos.shape[0] - N_META) // Q_BLOCK

    def to_blocks(a):
        a = a[:, N_META:].reshape(a.shape[0], n_blk, Q_BLOCK, *a.shape[2:])
        return jnp.moveaxis(a, 1, 0)

    blocks = tuple(to_blocks(a) for a in q_parts)
    out = lax.map(lambda args: attend(args[0], args[1]), (blocks, qpos[N_META:].reshape(n_blk, Q_BLOCK)))
    out = jnp.moveaxis(out, 0, 1)
    out = out.reshape(out.shape[0], n_blk * Q_BLOCK, *out.shape[3:])
    return jnp.concatenate([head, out], axis=1)


def merge(h, o_a, o_b, za, zb, ga, gb, lam_init, norm_g, w_oa, w_ob, w_o, ln_g, ln_b):
    b, t, _ = h.shape
    o_b = head_rms_norm(o_b, norm_g) * (1.0 - lam_init)
    br_a = (o_a.reshape(b, t, FOX_WIDTH) * jax.nn.silu(za)) @ w_oa
    br_b = (o_b.reshape(b, t, DIFF_WIDTH) * jax.nn.silu(zb)) @ w_ob
    m = jax.nn.sigmoid(ga) * br_a + jax.nn.sigmoid(gb) * br_b
    return layer_norm(ALPHA * h + m @ w_o, ln_g, ln_b)


def gather_pages(pool, layer, page_table):
    g = pool[layer, page_table]
    return g.reshape(g.shape[0], g.shape[1] * g.shape[2], *g.shape[3:])


def setup_inputs(seed: int = 0) -> dict:
    key = jax.random.key(seed)
    ks = jax.random.split(key, 24)
    f32 = jnp.float32
    n_pages = PAST_LEN // PAGE_SIZE
    n_used = DEC_BATCH * n_pages
    n_pool = n_used + (n_used + 3) // 4

    def nrm(k, shape, scale):
        return jax.random.normal(k, shape, f32) * scale

    page_table = jax.random.permutation(ks[0], n_pool)[:n_used].astype(jnp.int32).reshape(DEC_BATCH, n_pages)
    return {
        'x_prompt': nrm(ks[1], (BATCH, SEQ, D_MODEL), 1.0),
        'x_sample': nrm(ks[2], (DEC_BATCH, DEC_SEQ, D_MODEL), 1.0),
        'cache_fox_k': nrm(ks[3], (DEPTH, n_pool, PAGE_SIZE, FOX_HEADS, HEAD_DIM), 1.0),
        'cache_fox_v': nrm(ks[4], (DEPTH, n_pool, PAGE_SIZE, FOX_HEADS, HEAD_DIM), 1.0),
        'cache_fox_logf': jax.nn.log_sigmoid(FORGET_BIAS_INIT + nrm(ks[5], (DEPTH, n_pool, PAGE_SIZE, FOX_HEADS), 1.0)),
        'cache_diff_k': nrm(ks[6], (DEPTH, n_pool, PAGE_SIZE, DIFF_HEADS, 2, HEAD_DIM), 1.0),
        'cache_diff_v': nrm(ks[7], (DEPTH, n_pool, PAGE_SIZE, DIFF_HEADS, 2 * HEAD_DIM), 1.0),
        'page_table': page_table,
        'meta_tokens': nrm(ks[8], (N_META, D_MODEL), 1.0),
        'w_in': nrm(ks[9], (DEPTH, D_MODEL, N_IN), D_MODEL ** -0.5),
        'b_forget': FORGET_BIAS_INIT + nrm(ks[10], (DEPTH, FOX_HEADS), 0.5),
        'lambda_q1': nrm(ks[11], (DEPTH, HEAD_DIM), 0.1),
        'lambda_k1': nrm(ks[12], (DEPTH, HEAD_DIM), 0.1),
        'lambda_q2': nrm(ks[13], (DEPTH, HEAD_DIM), 0.1),
        'lambda_k2': nrm(ks[14], (DEPTH, HEAD_DIM), 0.1),
        'diff_norm_g': 1.0 + nrm(ks[15], (DEPTH, 2 * HEAD_DIM), 0.02),
        'w_out_fox': nrm(ks[16], (DEPTH, FOX_WIDTH, D_MODEL), BETA * FOX_WIDTH ** -0.5),
        'w_out_diff': nrm(ks[17], (DEPTH, DIFF_WIDTH, D_MODEL), BETA * DIFF_WIDTH ** -0.5),
        'w_out': nrm(ks[18], (DEPTH, D_MODEL, D_MODEL), BETA * D_MODEL ** -0.5),
        'ln_g': 1.0 + nrm(ks[19], (DEPTH, D_MODEL), 0.02),
        'ln_b': nrm(ks[20], (DEPTH, D_MODEL), 0.02),
    }


def reference(x_prompt, x_sample, cache_fox_k, cache_fox_v, cache_fox_logf, cache_diff_k, cache_diff_v,
              page_table, meta_tokens, w_in, b_forget, lambda_q1, lambda_k1, lambda_q2, lambda_k2,
              diff_norm_g, w_out_fox, w_out_diff, w_out, ln_g, ln_b):
    nb = x_prompt.shape[0]
    meta = jnp.broadcast_to(meta_tokens.astype(x_prompt.dtype)[None], (nb, N_META, D_MODEL))
    h = jnp.concatenate([meta, x_prompt], axis=1)
    pos = jnp.arange(h.shape[1], dtype=jnp.int32)
    p_fk, p_fv, p_fl, p_dk, p_dv = [], [], [], [], []
    for l in range(DEPTH):
        lam_init = lambda_init(l)
        lam = diff_lambda(lambda_q1[l], lambda_k1[l], lambda_q2[l], lambda_k2[l], lam_init)
        qa, ka, va, za, logf, qb, kb, vb, zb, ga, gb = in_project(h, w_in[l], b_forget[l], pos)
        c = jnp.cumsum(logf, axis=1)
        o_a = block_sweep(lambda qs, qp: fox_attend(qs[0], qs[1], qp, ka, va, c, pos), (qa, c), pos)
        o_b = block_sweep(lambda qs, qp: diff_attend(qs[0], qp, kb, vb, pos, lam), (qb,), pos)
        h = merge(h, o_a, o_b, za, zb, ga, gb, lam_init, diff_norm_g[l],
                  w_out_fox[l], w_out_diff[l], w_out[l], ln_g[l], ln_b[l])
        p_fk.append(ka)
        p_fv.append(va)
        p_fl.append(logf.astype(ka.dtype))
        p_dk.append(kb)
        p_dv.append(vb)
    y_prompt = h[:, N_META:]

    past = page_table.shape[1] * cache_fox_k.shape[2]
    hs = x_sample
    n_new = hs.shape[1]
    pos_s = past + jnp.arange(n_new, dtype=jnp.int32)
    kpos = jnp.arange(past + n_new, dtype=jnp.int32)
    s_fk, s_fv, s_fl, s_dk, s_dv = [], [], [], [], []
    for l in range(DEPTH):
        lam_init = lambda_init(l)
        lam = diff_lambda(lambda_q1[l], lambda_k1[l], lambda_q2[l], lambda_k2[l], lam_init)
        qa, ka, va, za, logf, qb, kb, vb, zb, ga, gb = in_project(hs, w_in[l], b_forget[l], pos_s)
        fk = jnp.concatenate([gather_pages(cache_fox_k, l, page_table).astype(ka.dtype), ka], axis=1)
        fv = jnp.concatenate([gather_pages(cache_fox_v, l, page_table).astype(va.dtype), va], axis=1)
        lf = jnp.concatenate([gather_pages(cache_fox_logf, l, page_table).astype(jnp.float32), logf], axis=1)
        c = jnp.cumsum(lf, axis=1)
        o_a = fox_attend(qa, c[:, past:], pos_s, fk, fv, c, kpos)
        dk = jnp.concatenate([gather_pages(cache_diff_k, l, page_table).astype(kb.dtype), kb], axis=1)
        dv = jnp.concatenate([gather_pages(cache_diff_v, l, page_table).astype(vb.dtype), vb], axis=1)
        o_b = diff_attend(qb, pos_s, dk, dv, kpos, lam)
        hs = merge(hs, o_a, o_b, za, zb, ga, gb, lam_init, diff_norm_g[l],
                   w_out_fox[l], w_out_diff[l], w_out[l], ln_g[l], ln_b[l])
        s_fk.append(ka)
        s_fv.append(va)
        s_fl.append(logf.astype(ka.dtype))
        s_dk.append(kb)
        s_dv.append(vb)
    y_sample = hs

    fox_k_prompt = jnp.stack(p_fk)
    fox_v_prompt = jnp.stack(p_fv)
    fox_logf_prompt = jnp.stack(p_fl)
    diff_k_prompt = jnp.stack(p_dk)
    diff_v_prompt = jnp.stack(p_dv)
    fox_k_sample = jnp.stack(s_fk)
    fox_v_sample = jnp.stack(s_fv)
    fox_logf_sample = jnp.stack(s_fl)
    diff_k_sample = jnp.stack(s_dk)
    diff_v_sample = jnp.stack(s_dv)
    return (y_prompt, y_sample, fox_k_prompt, fox_v_prompt, fox_logf_prompt, diff_k_prompt, diff_v_prompt,
            fox_k_sample, fox_v_sample, fox_logf_sample, diff_k_sample, diff_v_sample)
```

```python
import functools
import math

import jax
import jax.numpy as jnp
from jax import lax
from jax.experimental import pallas as pl
from jax.experimental.pallas import tpu as pltpu

F32 = jnp.float32
BF16 = jnp.bfloat16

HEAD_DIM = 128
N_META = 16
ROPE_THETA = 10000.0
EPS = 1e-5
LANES = 128
ROW_PAD = 16
V7X_VMEM_LIMIT = 56 * 1024 * 1024

_NT = (((1,), (1,)), ((), ()))


def _lambda_init(layer):
    return 0.8 - 0.6 * math.exp(-0.3 * layer)


def _split3(x):
    hi = x.astype(BF16)
    r1 = x - hi.astype(F32)
    mid = r1.astype(BF16)
    lo = (r1 - mid.astype(F32)).astype(BF16)
    return hi, mid, lo


def _dot3(a_bf16, x_f32):
    return sum(jnp.dot(a_bf16, p, preferred_element_type=F32) for p in _split3(x_f32))


def _dot3_left(x_f32, a_bf16):
    return sum(jnp.dot(p, a_bf16, preferred_element_type=F32) for p in _split3(x_f32))


def _diff_lambda(lq1, lk1, lq2, lk2, lam_init):
    e1 = jnp.exp(jnp.sum(lq1 * lk1, axis=1, keepdims=True))
    e2 = jnp.exp(jnp.sum(lq2 * lk2, axis=1, keepdims=True))
    return e1 - e2 + lam_init


def _mm_kernel(x_ref, w_ref, *rest, epilogue, n_out, alpha):
    outs = rest[len(rest) - n_out:]
    extra = rest[:len(rest) - n_out]
    acc = jnp.dot(x_ref[...], w_ref[...], preferred_element_type=F32)
    if epilogue == "residual":
        acc = alpha * extra[0][...] + acc
    if epilogue == "logsig":
        z = acc + extra[0][...]
        acc = jnp.minimum(z, 0.0) - jnp.log1p(jnp.exp(-jnp.abs(z)))
    if epilogue == "rope":
        cos = extra[0][...]
        sin = extra[1][...]
        for g in range(acc.shape[1] // HEAD_DIM):
            sl = slice(g * HEAD_DIM, (g + 1) * HEAD_DIM)
            xg = acc[:, sl]
            r = xg * cos + pltpu.roll(xg, HEAD_DIM // 2, 1) * sin
            for o in outs:
                o[:, sl] = r.astype(o.dtype)
    else:
        for o in outs:
            o[...] = acc.astype(o.dtype)


def _matmul(x, w, out_dtypes, *, bm, bn, epilogue="none", extra=(), alpha=None, name="mm"):
    m, k = x.shape
    n = w.shape[1]
    assert m % bm == 0 and n % bn == 0
    in_specs = [pl.BlockSpec((bm, k), lambda j, i: (i, 0)),
                pl.BlockSpec((k, bn), lambda j, i: (0, j))]
    if epilogue == "logsig":
        in_specs.append(pl.BlockSpec((1, bn), lambda j, i: (0, j)))
    elif epilogue == "rope":
        nt = extra[0].shape[0] // bm
        in_specs += [pl.BlockSpec((bm, HEAD_DIM), lambda j, i: (i % nt, 0))] * 2
    elif epilogue == "residual":
        in_specs.append(pl.BlockSpec((bm, bn), lambda j, i: (i, j)))
    outs = pl.pallas_call(
        functools.partial(_mm_kernel, epilogue=epilogue, n_out=len(out_dtypes), alpha=alpha),
        grid=(n // bn, m // bm),
        in_specs=in_specs,
        out_specs=[pl.BlockSpec((bm, bn), lambda j, i: (i, j)) for _ in out_dtypes],
        out_shape=[jax.ShapeDtypeStruct((m, n), d) for d in out_dtypes],
        compiler_params=pltpu.CompilerParams(
            dimension_semantics=("arbitrary", "arbitrary"), vmem_limit_bytes=V7X_VMEM_LIMIT),
        name=name,
    )(x, w, *extra)
    return outs


def _cum_kernel(lf_ref, c_ref, cts_ref, ctm_ref, *, n_chunks, chunk):
    def tri(n):
        r = lax.broadcasted_iota(jnp.int32, (n, n), 0)
        c = lax.broadcasted_iota(jnp.int32, (n, n), 1)
        return (c <= r).astype(BF16)

    c0 = _dot3(tri(LANES), lf_ref[0:LANES, :])
    c_ref[0:LANES, :] = c0
    ctm_ref[...] = c0.T[0:ctm_ref.shape[0], :]
    carry = c0[N_META - 1:N_META, :]
    tri_c = tri(chunk)
    for j in range(n_chunks):
        r0 = N_META + j * chunk
        cj = _dot3(tri_c, lf_ref[r0:r0 + chunk, :]) + carry
        c_ref[r0:r0 + chunk, :] = cj
        cts_ref[:, j * chunk:(j + 1) * chunk] = cj.T[0:cts_ref.shape[0], :]
        carry = cj[chunk - 1:chunk, :]


def _prompt_cumsum(logf_pad, n_heads, chunk):
    b, t, _ = logf_pad.shape
    n_chunks = (t - N_META) // chunk
    return pl.pallas_call(
        functools.partial(_cum_kernel, n_chunks=n_chunks, chunk=chunk),
        grid=(b,),
        in_specs=[pl.BlockSpec((None, t, LANES), lambda i: (i, 0, 0))],
        out_specs=[pl.BlockSpec((None, t, LANES), lambda i: (i, 0, 0)),
                   pl.BlockSpec((None, n_heads, t - N_META), lambda i: (i, 0, 0)),
                   pl.BlockSpec((None, n_heads, LANES), lambda i: (i, 0, 0))],
        out_shape=[jax.ShapeDtypeStruct((b, t, LANES), F32),
                   jax.ShapeDtypeStruct((b, n_heads, t - N_META), F32),
                   jax.ShapeDtypeStruct((b, n_heads, LANES), F32)],
        compiler_params=pltpu.CompilerParams(dimension_semantics=("arbitrary",)),
        name="prompt_cumsum",
    )(logf_pad)


def _softmax_step(state, s, v):
    m, l, acc = state
    m_new = jnp.maximum(m, jnp.max(s, axis=1, keepdims=True))
    alpha = jnp.exp(m - m_new)
    p = jnp.exp(s - m_new)
    l = alpha * l + jnp.sum(p, axis=1, keepdims=True)
    acc = alpha * acc + jnp.dot(p.astype(BF16), v, preferred_element_type=F32)
    return m_new, l, acc


def _init_state(n, dv):
    return (jnp.full((n, 1), -jnp.inf, F32), jnp.zeros((n, 1), F32), jnp.zeros((n, dv), F32))


def _causal_mask(n):
    r = lax.broadcasted_iota(jnp.int32, (n, n), 0)
    c = lax.broadcasted_iota(jnp.int32, (n, n), 1)
    return c <= r


def _fox_kernel(q_ref, k_ref, v_ref, z_ref, c_ref, cts_ref, ctm_ref, o_ref, *, scale, n_blk, blk):
    h = pl.program_id(1)

    def cq_col(r0, n):
        cb = c_ref[r0:r0 + n, :]
        lane = lax.broadcasted_iota(jnp.int32, cb.shape, 1)
        return jnp.sum(jnp.where(lane == h, cb, 0.0), axis=1, keepdims=True)

    def scores(q, cq, k, ck):
        return lax.dot_general(q, k, _NT, preferred_element_type=F32) * scale + (cq - ck)

    def finish(state, r0, n_rows, n_store):
        _, l, acc = state
        o = acc / l
        z = z_ref[r0:r0 + n_rows, :]
        x = o * (z * jax.nn.sigmoid(z))
        o_ref[r0:r0 + n_store, :] = x[0:n_store].astype(o_ref.dtype)

    k_head = k_ref[0:LANES, :]
    v_head = v_ref[0:LANES, :]
    ck_head = ctm_ref[...]
    s = scores(q_ref[0:LANES, :], cq_col(0, LANES), k_head, ck_head)
    s = jnp.where(_causal_mask(LANES), s, -jnp.inf)
    finish(_softmax_step(_init_state(LANES, HEAD_DIM), s, v_head), 0, LANES, N_META)

    meta_cols = lax.broadcasted_iota(jnp.int32, (blk, LANES), 1) < N_META
    diag = _causal_mask(blk)
    for i in range(n_blk):
        r0 = N_META + i * blk
        q = q_ref[r0:r0 + blk, :]
        cq = cq_col(r0, blk)
        s = jnp.where(meta_cols, scores(q, cq, k_head, ck_head), -jnp.inf)
        state = _softmax_step(_init_state(blk, HEAD_DIM), s, v_head)

        def body(j, st, q=q, cq=cq):
            k0 = pl.multiple_of(N_META + j * blk, ROW_PAD)
            c0 = pl.multiple_of(j * blk, blk)
            sj = scores(q, cq, k_ref[pl.ds(k0, blk), :], cts_ref[:, pl.ds(c0, blk)])
            return _softmax_step(st, sj, v_ref[pl.ds(k0, blk), :])

        state = lax.fori_loop(0, i, body, state)
        s = scores(q, cq, k_ref[r0:r0 + blk, :], cts_ref[:, i * blk:(i + 1) * blk])
        s = jnp.where(diag, s, -jnp.inf)
        finish(_softmax_step(state, s, v_ref[r0:r0 + blk, :]), r0, blk, blk)


def _fox_attention(q, k, v, zg, c, cts, ctm, *, n_heads, blk):
    b, t, _ = q.shape
    n_blk = (t - N_META) // blk
    hd = pl.BlockSpec((None, t, HEAD_DIM), lambda i, h: (i, 0, h))
    return pl.pallas_call(
        functools.partial(_fox_kernel, scale=HEAD_DIM ** -0.5, n_blk=n_blk, blk=blk),
        grid=(b, n_heads),
        in_specs=[hd, hd, hd, hd,
                  pl.BlockSpec((None, t, LANES), lambda i, h: (i, 0, 0)),
                  pl.BlockSpec((None, None, 1, t - N_META), lambda i, h: (i, h, 0, 0)),
                  pl.BlockSpec((None, None, 1, LANES), lambda i, h: (i, h, 0, 0))],
        out_specs=hd,
        out_shape=jax.ShapeDtypeStruct((b, t, n_heads * HEAD_DIM), BF16),
        compiler_params=pltpu.CompilerParams(dimension_semantics=("arbitrary", "arbitrary")),
        name="fox_attention",
    )(q, k, v, zg, c, cts.reshape(b, n_heads, 1, t - N_META), ctm.reshape(b, n_heads, 1, LANES))


def _diff_kernel(q_ref, k_ref, v_ref, z_ref, g_ref, lq1, lk1, lq2, lk2, o_ref, *,
                 scale, n_blk, blk, lam_init):
    dv = 2 * HEAD_DIM
    lam = _diff_lambda(lq1[...], lk1[...], lq2[...], lk2[...], lam_init)

    def scores(q, k):
        return lax.dot_general(q, k, _NT, preferred_element_type=F32) * scale

    def both(states, q_rows, k_rows, mask, v):
        out = []
        for c in range(2):
            sl = slice(c * HEAD_DIM, (c + 1) * HEAD_DIM)
            s = scores(q_rows[:, sl], k_rows[:, sl])
            if mask is not None:
                s = jnp.where(mask, s, -jnp.inf)
            out.append(_softmax_step(states[c], s, v))
        return tuple(out)

    def finish(states, r0, n_rows, n_store):
        (_, l0, a0), (_, l1, a1) = states
        o = a0 / l0 - lam * (a1 / l1)
        o = o * lax.rsqrt(jnp.mean(o * o, axis=1, keepdims=True) + EPS) * g_ref[...]
        o = o * (1.0 - lam_init)
        z = z_ref[r0:r0 + n_rows, :]
        x = o * (z * jax.nn.sigmoid(z))
        o_ref[r0:r0 + n_store, :] = x[0:n_store].astype(o_ref.dtype)

    k_head = k_ref[0:LANES, :]
    v_head = v_ref[0:LANES, :]
    st = (_init_state(LANES, dv), _init_state(LANES, dv))
    finish(both(st, q_ref[0:LANES, :], k_head, _causal_mask(LANES), v_head), 0, LANES, N_META)

    meta_cols = lax.broadcasted_iota(jnp.int32, (blk, LANES), 1) < N_META
    diag = _causal_mask(blk)
    for i in range(n_blk):
        r0 = N_META + i * blk
        q = q_ref[r0:r0 + blk, :]
        st = both((_init_state(blk, dv), _init_state(blk, dv)), q, k_head, meta_cols, v_head)

        def body(j, st, q=q):
            k0 = pl.multiple_of(N_META + j * blk, ROW_PAD)
            return both(st, q, k_ref[pl.ds(k0, blk), :], None, v_ref[pl.ds(k0, blk), :])

        st = lax.fori_loop(0, i, body, st)
        st = both(st, q, k_ref[r0:r0 + blk, :], diag, v_ref[r0:r0 + blk, :])
        finish(st, r0, blk, blk)


def _diff_attention(q, k, v, zg, z_col0, g, lams, *, n_heads, blk, lam_init):
    b, t, _ = q.shape
    dv = 2 * HEAD_DIM
    n_blk = (t - N_META) // blk
    zoff = z_col0 // dv
    hd = pl.BlockSpec((None, t, dv), lambda i, h: (i, 0, h))
    vec = pl.BlockSpec((1, HEAD_DIM), lambda i, h: (0, 0))
    return pl.pallas_call(
        functools.partial(_diff_kernel, scale=HEAD_DIM ** -0.5, n_blk=n_blk, blk=blk,
                          lam_init=lam_init),
        grid=(b, n_heads),
        in_specs=[hd, hd, hd,
                  pl.BlockSpec((None, t, dv), lambda i, h: (i, 0, h + zoff)),
                  pl.BlockSpec((1, dv), lambda i, h: (0, 0)),
                  vec, vec, vec, vec],
        out_specs=hd,
        out_shape=jax.ShapeDtypeStruct((b, t, n_heads * dv), BF16),
        compiler_params=pltpu.CompilerParams(dimension_semantics=("arbitrary", "arbitrary")),
        name="diff_attention",
    )(q, k, v, zg, g, *lams)


def _gate_kernel(xa_ref, xb_ref, wa_ref, wb_ref, ga_ref, gb_ref, o_ref):
    br_a = jnp.dot(xa_ref[...], wa_ref[...], preferred_element_type=F32)
    br_b = jnp.dot(xb_ref[...], wb_ref[...], preferred_element_type=F32)
    m = jax.nn.sigmoid(ga_ref[...]) * br_a + jax.nn.sigmoid(gb_ref[...]) * br_b
    o_ref[...] = m.astype(o_ref.dtype)


def _gated_merge(xa, xb, w_oa, w_ob, zg, ga_col0, gb_col0, *, bm, bn):
    m, k = xa.shape
    n = w_oa.shape[1]
    xs = pl.BlockSpec((bm, k), lambda j, i: (i, 0))
    ws = pl.BlockSpec((k, bn), lambda j, i: (0, j))
    ga_off, gb_off = ga_col0 // bn, gb_col0 // bn
    return pl.pallas_call(
        _gate_kernel,
        grid=(n // bn, m // bm),
        in_specs=[xs, xs, ws, ws,
                  pl.BlockSpec((bm, bn), lambda j, i: (i, j + ga_off)),
                  pl.BlockSpec((bm, bn), lambda j, i: (i, j + gb_off))],
        out_specs=pl.BlockSpec((bm, bn), lambda j, i: (i, j)),
        out_shape=jax.ShapeDtypeStruct((m, n), BF16),
        compiler_params=pltpu.CompilerParams(
            dimension_semantics=("arbitrary", "arbitrary"), vmem_limit_bytes=V7X_VMEM_LIMIT),
        name="gated_merge",
    )(xa, xb, w_oa, w_ob, zg, zg)


def _ln_kernel(u_ref, g_ref, b_ref, y_ref, yb_ref):
    u = u_ref[...]
    mu = jnp.mean(u, axis=1, keepdims=True)
    d = u - mu
    var = jnp.mean(d * d, axis=1, keepdims=True)
    y = d * lax.rsqrt(var + EPS) * g_ref[...] + b_ref[...]
    y_ref[...] = y
    yb_ref[...] = y.astype(yb_ref.dtype)


def _layer_norm(u, ln_g, ln_b, *, bm):
    m, n = u.shape
    row = pl.BlockSpec((bm, n), lambda i: (i, 0))
    vec = pl.BlockSpec((1, n), lambda i: (0, 0))
    return pl.pallas_call(
        _ln_kernel,
        grid=(m // bm,),
        in_specs=[row, vec, vec],
        out_specs=[row, row],
        out_shape=[jax.ShapeDtypeStruct((m, n), F32), jax.ShapeDtypeStruct((m, n), BF16)],
        compiler_params=pltpu.CompilerParams(dimension_semantics=("arbitrary",)),
        name="layer_norm",
    )(u, ln_g, ln_b)


def _out_ln(m_bf16, w_o, h, ln_g, ln_b, *, bm, bm_ln, alpha):
    u, = _matmul(m_bf16, w_o, [F32], bm=bm, bn=1024, epilogue="residual", extra=(h,), alpha=alpha,
                 name="out_proj")
    return _layer_norm(u, ln_g, ln_b, bm=bm_ln)


def _decode_bias_kernel(pt_ref, pool_ref, new_ref, o_ref, g_ref, *, n_pages, n_heads):
    b = pl.program_id(0)
    w = pool_ref.shape[1]

    def gather(j, carry):
        g_ref[pl.ds(j, 1), :] = pool_ref[pl.ds(pt_ref[b, j], 1), :]
        return carry

    lax.fori_loop(0, n_pages, gather, 0)

    r = lax.broadcasted_iota(jnp.int32, (LANES, LANES), 0)
    c = lax.broadcasted_iota(jnp.int32, (LANES, LANES), 1)
    same_head = (r % n_heads) == (c % n_heads)
    head_sum = same_head.astype(BF16)
    later_in_block = (same_head & (r > c)).astype(BF16)
    after = jnp.zeros((n_pages, LANES), F32)
    for kb in reversed(range(w // LANES)):
        sl = slice(kb * LANES, (kb + 1) * LANES)
        gk = g_ref[:, sl]
        o_ref[:, sl] = _dot3_left(gk, later_in_block) + _dot3_left(after, head_sum)
        after = after + gk
    total = _dot3_left(after, head_sum)
    pr = lax.broadcasted_iota(jnp.int32, (n_pages, n_pages), 0)
    pc = lax.broadcasted_iota(jnp.int32, (n_pages, n_pages), 1)
    rest = _dot3((pc > pr).astype(BF16), total) + new_ref[pl.ds(b, 1), :]
    for kb in range(w // LANES):
        sl = slice(kb * LANES, (kb + 1) * LANES)
        o_ref[:, sl] = o_ref[:, sl] + rest


def _decode_bias(page_table, pool_flat, logf_new_pad, *, n_heads):
    db, n_pages = page_table.shape
    n_pool, w = pool_flat.shape
    return pl.pallas_call(
        functools.partial(_decode_bias_kernel, n_pages=n_pages, n_heads=n_heads),
        grid_spec=pltpu.PrefetchScalarGridSpec(
            num_scalar_prefetch=1,
            grid=(db,),
            in_specs=[pl.BlockSpec((n_pool, w), lambda b, pt: (0, 0)),
                      pl.BlockSpec(logf_new_pad.shape, lambda b, pt: (0, 0))],
            out_specs=pl.BlockSpec((None, n_pages, w), lambda b, pt: (b, 0, 0)),
            scratch_shapes=[pltpu.VMEM((n_pages, w), F32)]),
        out_shape=jax.ShapeDtypeStruct((db, n_pages, w), F32),
        compiler_params=pltpu.CompilerParams(
            dimension_semantics=("arbitrary",), vmem_limit_bytes=V7X_VMEM_LIMIT),
        name="decode_bias",
    )(page_table, pool_flat, logf_new_pad)


def _decode_kernel(pt_ref, fk_ref, fv_ref, dk_ref, dv_ref, bias_ref,
                   qf_ref, knf_ref, vnf_ref, qd_ref, knd_ref, vnd_ref,
                   zf_ref, zd_ref, g_ref, lq1, lk1, lq2, lk2,
                   xa_ref, xb_ref,
                   mf, lf, af, md, ld, ad, *, scale, lam_init, n_heads_f, n_heads_d):
    j = pl.program_id(1)
    hp = 2 * n_heads_d

    def bf(x):
        return x.astype(BF16)

    @pl.when(j == 0)
    def _():
        qn = bf(qf_ref[...]).astype(F32)
        s_new = jnp.sum(qn * bf(knf_ref[...]).astype(F32), axis=1, keepdims=True) * scale
        mf[...] = s_new
        lf[...] = jnp.ones_like(lf)
        af[...] = bf(vnf_ref[...]).astype(F32)
        for c in range(2):
            qn = bf(qd_ref[c]).astype(F32)
            s_new = jnp.sum(qn * bf(knd_ref[c]).astype(F32), axis=1, keepdims=True) * scale
            md[c * hp:(c + 1) * hp, :] = s_new
            ad[c * hp:(c + 1) * hp, :] = bf(vnd_ref[...]).astype(F32)
        ld[...] = jnp.ones_like(ld)

    s = lax.dot_general(bf(qf_ref[...]), bf(fk_ref[...]), _NT, preferred_element_type=F32)
    s = s * scale + bias_ref[pl.ds(j % 8, 1), :]
    row = lax.broadcasted_iota(jnp.int32, s.shape, 0)
    col = lax.broadcasted_iota(jnp.int32, s.shape, 1)
    s = jnp.where((col % n_heads_f) == row, s, -jnp.inf)
    m_old = mf[...]
    m_new = jnp.maximum(m_old, jnp.max(s, axis=1, keepdims=True))
    alpha = jnp.exp(m_old - m_new)
    p = jnp.exp(s - m_new)
    mf[...] = m_new
    lf[...] = alpha * lf[...] + jnp.sum(p, axis=1, keepdims=True)
    af[...] = alpha * af[...] + jnp.dot(bf(p), bf(fv_ref[...]), preferred_element_type=F32)

    n_kh = dk_ref.shape[0] // 2
    parts = []
    for c in range(2):
        kc = bf(dk_ref[pl.ds(c, n_kh, stride=2), :])
        parts.append(lax.dot_general(bf(qd_ref[c]), kc, _NT, preferred_element_type=F32))
    s = jnp.concatenate(parts, axis=0) * scale
    row = lax.broadcasted_iota(jnp.int32, s.shape, 0)
    col = lax.broadcasted_iota(jnp.int32, s.shape, 1)
    s = jnp.where((col % n_heads_d) == (row % hp), s, -jnp.inf)
    m_old = md[...]
    m_new = jnp.maximum(m_old, jnp.max(s, axis=1, keepdims=True))
    alpha = jnp.exp(m_old - m_new)
    p = jnp.exp(s - m_new)
    md[...] = m_new
    ld[...] = alpha * ld[...] + jnp.sum(p, axis=1, keepdims=True)
    ad[...] = alpha * ad[...] + jnp.dot(bf(p), bf(dv_ref[...]), preferred_element_type=F32)

    @pl.when(j == pl.num_programs(1) - 1)
    def _():
        z = zf_ref[...]
        xa_ref[...] = ((af[...] / lf[...]) * (z * jax.nn.sigmoid(z))).astype(xa_ref.dtype)
        lam = _diff_lambda(lq1[...], lk1[...], lq2[...], lk2[...], lam_init)
        o = ad[0:hp, :] / ld[0:hp, :] - lam * (ad[hp:2 * hp, :] / ld[hp:2 * hp, :])
        o = o * lax.rsqrt(jnp.mean(o * o, axis=1, keepdims=True) + EPS) * g_ref[...]
        o = o * (1.0 - lam_init)
        z = zd_ref[...]
        xb_ref[...] = (o * (z * jax.nn.sigmoid(z))).astype(xb_ref.dtype)


def _decode_attention(page_table, fk, fv, dk, dv, bias, qf, knf, vnf, qd, knd, vnd, zf, zd,
                      g, lams, *, layer, n_pool, lam_init):
    db, n_pages = page_table.shape
    n_heads_f = qf.shape[1]
    hp = qd.shape[2]
    n_heads_d = hp // 2
    dvw = 2 * HEAD_DIM
    base = layer * n_pool

    def page(shape):
        return pl.BlockSpec((None,) + shape, lambda b, j, pt: (base + pt[b, j], 0, 0))

    def per_b(shape):
        nd = len(shape)
        return pl.BlockSpec((None,) + shape, lambda b, j, pt: (b,) + (0,) * nd)

    vec = pl.BlockSpec((1, HEAD_DIM), lambda b, j, pt: (0, 0))
    return pl.pallas_call(
        functools.partial(_decode_kernel, scale=HEAD_DIM ** -0.5, lam_init=lam_init,
                          n_heads_f=n_heads_f, n_heads_d=n_heads_d),
        grid_spec=pltpu.PrefetchScalarGridSpec(
            num_scalar_prefetch=1,
            grid=(db, n_pages),
            in_specs=[page(fk.shape[1:]), page(fv.shape[1:]), page(dk.shape[1:]), page(dv.shape[1:]),
                      pl.BlockSpec((None, 8, bias.shape[2]), lambda b, j, pt: (b, j // 8, 0)),
                      per_b(qf.shape[1:]), per_b(knf.shape[1:]), per_b(vnf.shape[1:]),
                      per_b(qd.shape[1:]), per_b(knd.shape[1:]), per_b(vnd.shape[1:]),
                      per_b(zf.shape[1:]), per_b(zd.shape[1:]),
                      pl.BlockSpec((1, dvw), lambda b, j, pt: (0, 0)),
                      vec, vec, vec, vec],
            out_specs=[per_b((n_heads_f, HEAD_DIM)), per_b((hp, dvw))],
            scratch_shapes=[pltpu.VMEM((n_heads_f, 1), F32), pltpu.VMEM((n_heads_f, 1), F32),
                            pltpu.VMEM((n_heads_f, HEAD_DIM), F32),
                            pltpu.VMEM((2 * hp, 1), F32), pltpu.VMEM((2 * hp, 1), F32),
                            pltpu.VMEM((2 * hp, dvw), F32)]),
        out_shape=[jax.ShapeDtypeStruct((db, n_heads_f, HEAD_DIM), BF16),
                   jax.ShapeDtypeStruct((db, hp, dvw), BF16)],
        compiler_params=pltpu.CompilerParams(
            dimension_semantics=("arbitrary", "arbitrary"), vmem_limit_bytes=V7X_VMEM_LIMIT),
        name="decode_attention",
    )(page_table, fk, fv, dk, dv, bias, qf, knf, vnf, qd, knd, vnd, zf, zd, g, *lams)


def _rope_tables(pos):
    half = HEAD_DIM // 2
    inv = 1.0 / (ROPE_THETA ** (jnp.arange(half, dtype=F32) * 2.0 / HEAD_DIM))
    ang = pos.astype(F32)[:, None] * inv[None, :]
    cos, sin = jnp.cos(ang), jnp.sin(ang)
    return jnp.concatenate([cos, cos], axis=1), jnp.concatenate([-sin, sin], axis=1)


def _in_project(x, w, b_f, cos2, sin2, *, bm, fw, dw, d, n_heads_f):
    o = 0
    seg = {}
    for name, width in (("qa", fw), ("ka", fw), ("va", fw), ("za", fw), ("fa", n_heads_f),
                        ("qb", dw), ("kb", dw), ("vb", dw), ("zb", dw), ("ga", d), ("gb", d)):
        seg[name] = (o, o + width)
        o += width

    def cols(name):
        lo, hi = seg[name]
        return w[:, lo:hi].astype(BF16)

    mm = functools.partial(_matmul, x, bm=bm)
    bn = 1024
    qa, = mm(cols("qa"), [BF16], bn=bn, name="proj_qa")
    ka, ka16 = mm(cols("ka"), [F32, BF16], bn=bn, name="proj_ka")
    va, va16 = mm(cols("va"), [F32, BF16], bn=bn, name="proj_va")
    w_f = jnp.pad(cols("fa"), ((0, 0), (0, LANES - n_heads_f)))
    b_pad = jnp.pad(b_f.astype(F32), (0, LANES - n_heads_f)).reshape(1, LANES)
    logf_pad, = mm(w_f, [F32], bn=LANES, epilogue="logsig", extra=(b_pad,), name="proj_fa")
    qb, = mm(cols("qb"), [BF16], bn=bn, epilogue="rope", extra=(cos2, sin2), name="proj_qb")
    kb, kb16 = mm(cols("kb"), [F32, BF16], bn=bn, epilogue="rope", extra=(cos2, sin2), name="proj_kb")
    vb, vb16 = mm(cols("vb"), [F32, BF16], bn=bn, name="proj_vb")
    w_g = jnp.concatenate([cols("za"), cols("zb"), cols("ga"), cols("gb")], axis=1)
    zg, = mm(w_g, [F32], bn=bn, name="proj_gates")
    return dict(qa=qa, ka=ka, ka16=ka16, va=va, va16=va16, logf_pad=logf_pad, qb=qb, kb=kb,
                kb16=kb16, vb=vb, vb16=vb16, zg=zg)


def kernel(x_prompt, x_sample, cache_fox_k, cache_fox_v, cache_fox_logf, cache_diff_k, cache_diff_v,
           page_table, meta_tokens, w_in, b_forget, lambda_q1, lambda_k1, lambda_q2, lambda_k2,
           diff_norm_g, w_out_fox, w_out_diff, w_out, ln_g, ln_b):
    nb, seq, d = x_prompt.shape
    depth = w_in.shape[0]
    t = seq + N_META
    n_heads_f = b_forget.shape[1]
    fw = n_heads_f * HEAD_DIM
    dw = w_out_diff.shape[1]
    n_heads_d = dw // (2 * HEAD_DIM)
    db, n_new, _ = x_sample.shape
    n_pool, page = cache_fox_k.shape[1], cache_fox_k.shape[2]
    n_pages = page_table.shape[1]
    past = n_pages * page
    assert n_new == 1 and db <= ROW_PAD and n_pages % 8 == 0
    alpha = (2 * depth) ** 0.25
    blk = 256
    bm = t // 3
    bm_ln = max(r for r in range(ROW_PAD, 193, ROW_PAD) if (nb * t) % r == 0)
    assert t % 3 == 0 and bm % ROW_PAD == 0 and (t - N_META) % blk == 0

    cos_p, sin_p = _rope_tables(jnp.arange(t, dtype=jnp.int32))
    cos_s, sin_s = _rope_tables(jnp.full((ROW_PAD,), past, dtype=jnp.int32))

    meta = jnp.broadcast_to(meta_tokens.astype(x_prompt.dtype)[None], (nb, N_META, d))
    h = jnp.concatenate([meta, x_prompt], axis=1).reshape(nb * t, d)
    h16 = h.astype(BF16)
    hs = jnp.pad(x_sample.reshape(db, d), ((0, ROW_PAD - db), (0, 0)))
    hs16 = hs.astype(BF16)

    fk_pool = cache_fox_k.reshape(depth * n_pool, page * n_heads_f, HEAD_DIM)
    fv_pool = cache_fox_v.reshape(depth * n_pool, page * n_heads_f, HEAD_DIM)
    dk_pool = cache_diff_k.reshape(depth * n_pool, page * n_heads_d * 2, HEAD_DIM)
    dv_pool = cache_diff_v.reshape(depth * n_pool, page * n_heads_d, 2 * HEAD_DIM)
    lf_pool = cache_fox_logf.astype(F32).reshape(depth, n_pool, page * n_heads_f)

    outs_p = {k: [] for k in ("ka", "va", "lf", "kb", "vb")}
    outs_s = {k: [] for k in ("ka", "va", "lf", "kb", "vb")}
    row = lambda a, l: a[l].astype(F32).reshape(1, -1)

    for l in range(depth):
        lam_init = _lambda_init(l)
        lams = (row(lambda_q1, l), row(lambda_k1, l), row(lambda_q2, l), row(lambda_k2, l))
        g = row(diff_norm_g, l)
        w_oa = w_out_fox[l].astype(BF16)
        w_ob = w_out_diff[l].astype(BF16)
        w_o = w_out[l].astype(BF16)
        lng, lnb = row(ln_g, l), row(ln_b, l)
        proj = functools.partial(_in_project, w=w_in[l], b_f=b_forget[l], fw=fw, dw=dw, d=d,
                                 n_heads_f=n_heads_f)

        p = proj(h16, cos2=cos_p, sin2=sin_p, bm=bm)
        r3 = lambda a: a.reshape(nb, t, -1)
        c, cts, ctm = _prompt_cumsum(r3(p["logf_pad"]), n_heads_f, blk)
        zg3 = r3(p["zg"])
        xa = _fox_attention(r3(p["qa"]), r3(p["ka16"]), r3(p["va16"]), zg3, c, cts, ctm,
                            n_heads=n_heads_f, blk=blk)
        xb = _diff_attention(r3(p["qb"]), r3(p["kb16"]), r3(p["vb16"]), zg3, fw, g, lams,
                             n_heads=n_heads_d, blk=blk, lam_init=lam_init)
        m = _gated_merge(xa.reshape(nb * t, fw), xb.reshape(nb * t, dw), w_oa, w_ob, p["zg"],
                         fw + dw, fw + dw + d, bm=bm, bn=1024)
        h, h16 = _out_ln(m, w_o, h, lng, lnb, bm=bm, bm_ln=bm_ln, alpha=alpha)
        outs_p["ka"].append(p["ka"].reshape(nb, t, n_heads_f, HEAD_DIM))
        outs_p["va"].append(p["va"].reshape(nb, t, n_heads_f, HEAD_DIM))
        outs_p["lf"].append(p["logf_pad"][:, :n_heads_f].reshape(nb, t, n_heads_f))
        outs_p["kb"].append(p["kb"].reshape(nb, t, n_heads_d, 2, HEAD_DIM))
        outs_p["vb"].append(p["vb"].reshape(nb, t, n_heads_d, 2 * HEAD_DIM))

        s = proj(hs16, cos2=cos_s, sin2=sin_s, bm=ROW_PAD)
        lf_new = jnp.tile(s["logf_pad"][:, :n_heads_f], (1, LANES // n_heads_f))
        bias = _decode_bias(page_table, lf_pool[l], lf_new, n_heads=n_heads_f)
        zs = s["zg"]
        hp = 2 * n_heads_d

        def maps(a):
            a = a.reshape(ROW_PAD, n_heads_d, 2, HEAD_DIM).transpose(0, 2, 1, 3)
            return jnp.pad(a, ((0, 0), (0, 0), (0, hp - n_heads_d), (0, 0)))

        def vrows(a):
            a = a.reshape(ROW_PAD, n_heads_d, 2 * HEAD_DIM)
            return jnp.pad(a, ((0, 0), (0, hp - n_heads_d), (0, 0)))

        xa_s, xb_s = _decode_attention(
            page_table, fk_pool, fv_pool, dk_pool, dv_pool, bias,
            s["qa"].astype(F32).reshape(ROW_PAD, n_heads_f, HEAD_DIM),
            s["ka"].reshape(ROW_PAD, n_heads_f, HEAD_DIM),
            s["va"].reshape(ROW_PAD, n_heads_f, HEAD_DIM),
            maps(s["qb"].astype(F32)), maps(s["kb"]), vrows(s["vb"]),
            zs[:, :fw].reshape(ROW_PAD, n_heads_f, HEAD_DIM), vrows(zs[:, fw:fw + dw]),
            g, lams, layer=l, n_pool=n_pool, lam_init=lam_init)
        xa_s = jnp.pad(xa_s.reshape(db, fw), ((0, ROW_PAD - db), (0, 0)))
        xb_s = jnp.pad(xb_s[:, :n_heads_d].reshape(db, dw), ((0, ROW_PAD - db), (0, 0)))
        ms = _gated_merge(xa_s, xb_s, w_oa, w_ob, zs, fw + dw, fw + dw + d, bm=ROW_PAD, bn=1024)
        hs, hs16 = _out_ln(ms, w_o, hs, lng, lnb, bm=ROW_PAD, bm_ln=ROW_PAD, alpha=alpha)
        outs_s["ka"].append(s["ka"][:db].reshape(db, 1, n_heads_f, HEAD_DIM))
        outs_s["va"].append(s["va"][:db].reshape(db, 1, n_heads_f, HEAD_DIM))
        outs_s["lf"].append(s["logf_pad"][:db, :n_heads_f].reshape(db, 1, n_heads_f))
        outs_s["kb"].append(s["kb"][:db].reshape(db, 1, n_heads_d, 2, HEAD_DIM))
        outs_s["vb"].append(s["vb"][:db].reshape(db, 1, n_heads_d, 2 * HEAD_DIM))

    y_prompt = h.reshape(nb, t, d)[:, N_META:]
    y_sample = hs[:db].reshape(db, 1, d)
    st = jnp.stack
    return (y_prompt, y_sample,
            st(outs_p["ka"]), st(outs_p["va"]), st(outs_p["lf"]), st(outs_p["kb"]), st(outs_p["vb"]),
            st(outs_s["ka"]), st(outs_s["va"]), st(outs_s["lf"]), st(outs_s["kb"]), st(outs_s["vb"]))
```

```python
import functools
import math

import jax
import jax.numpy as jnp
from jax import lax
from jax.experimental import pallas as pl
from jax.experimental.pallas import tpu as pltpu

F32 = jnp.float32
BF16 = jnp.bfloat16

HEAD_DIM = 128
N_META = 16
ROPE_THETA = 10000.0
EPS = 1e-5
LANES = 128
ROW_PAD = 16
V7X_VMEM_LIMIT = 56 * 1024 * 1024

_NT = (((1,), (1,)), ((), ()))


def _lambda_init(layer):
    return 0.8 - 0.6 * math.exp(-0.3 * layer)


def _split3(x):
    hi = x.astype(BF16)
    r1 = x - hi.astype(F32)
    mid = r1.astype(BF16)
    lo = (r1 - mid.astype(F32)).astype(BF16)
    return hi, mid, lo


def _dot3(a_bf16, x_f32):
    return sum(jnp.dot(a_bf16, p, preferred_element_type=F32) for p in _split3(x_f32))


def _dot3_left(x_f32, a_bf16):
    return sum(jnp.dot(p, a_bf16, preferred_element_type=F32) for p in _split3(x_f32))


def _diff_lambda(lq1, lk1, lq2, lk2, lam_init):
    e1 = jnp.exp(jnp.sum(lq1 * lk1, axis=1, keepdims=True))
    e2 = jnp.exp(jnp.sum(lq2 * lk2, axis=1, keepdims=True))
    return e1 - e2 + lam_init


def _resident_bf16(w_ref, w16_ref):
    @pl.when(pl.program_id(1) == 0)
    def _():
        w16_ref[...] = w_ref[...].astype(BF16)

    return w16_ref[...]


def _mm_kernel(x_ref, w_ref, *rest, epilogue, n_out, alpha, w_rows, shift):
    w16_ref = rest[-1]
    rest = rest[:-1]
    if shift:
        w_next_ref = rest[0]
        rest = rest[1:]
        bn = w16_ref.shape[0]

        @pl.when(pl.program_id(1) == 0)
        def _():
            w16_ref[0:bn - shift, :] = w_ref[shift:bn, :].astype(BF16)
            w16_ref[bn - shift:bn, :] = w_next_ref[...].astype(BF16)

        w = w16_ref[...]
    else:
        w = _resident_bf16(w_ref, w16_ref)
    outs = rest[len(rest) - n_out:]
    extra = rest[:len(rest) - n_out]
    if w_rows:
        acc = lax.dot_general(x_ref[...], w, _NT, preferred_element_type=F32)
    else:
        acc = jnp.dot(x_ref[...], w, preferred_element_type=F32)
    if epilogue == "residual":
        acc = alpha * extra[0][...] + acc
    if epilogue == "logsig":
        z = acc + extra[0][...]
        acc = jnp.minimum(z, 0.0) - jnp.log1p(jnp.exp(-jnp.abs(z)))
    if epilogue == "rope":
        cos = extra[0][...]
        sin = extra[1][...]
        for g in range(acc.shape[1] // HEAD_DIM):
            sl = slice(g * HEAD_DIM, (g + 1) * HEAD_DIM)
            xg = acc[:, sl]
            r = xg * cos + pltpu.roll(xg, HEAD_DIM // 2, 1) * sin
            for o in outs:
                o[:, sl] = r.astype(o.dtype)
    else:
        for o in outs:
            o[...] = acc.astype(o.dtype)


def _matmul(x, w, layer, out_dtypes, *, bm, bn, n, w0=0, w_rows=False, epilogue="none", extra=(),
            alpha=None, name="mm", stack_depth=0, stacked=None):
    m, k = x.shape
    shift = w0 % bn
    assert m % bm == 0 and n % bn == 0 and shift in (0, ROW_PAD) and (w_rows or not shift)
    b0 = w0 // bn
    if w_rows:
        w_specs = [pl.BlockSpec((None, bn, k), lambda j, i: (layer, b0 + j, 0))]
        if shift:
            per = bn // shift
            w_specs.append(pl.BlockSpec((None, shift, k), lambda j, i: (layer, (b0 + j + 1) * per, 0)))
        w16_shape = (bn, k)
    else:
        w_specs = [pl.BlockSpec((None, k, bn), lambda j, i: (layer, 0, b0 + j))]
        w16_shape = (k, bn)
    in_specs = [pl.BlockSpec((bm, k), lambda j, i: (i, 0))] + w_specs
    if epilogue == "logsig":
        in_specs.append(pl.BlockSpec((1, bn), lambda j, i: (0, j)))
    elif epilogue == "rope":
        nt = extra[0].shape[0] // bm
        in_specs += [pl.BlockSpec((bm, HEAD_DIM), lambda j, i: (i % nt, 0))] * 2
    elif epilogue == "residual":
        in_specs.append(pl.BlockSpec((bm, bn), lambda j, i: (i, j)))
    out_specs = [pl.BlockSpec((bm, bn), lambda j, i: (i, j)) for _ in out_dtypes]
    out_shape = [jax.ShapeDtypeStruct((m, n), d) for d in out_dtypes]
    aliases, stacked_in = {}, ()
    if stack_depth:
        r0 = layer * (m // bm)
        out_specs[0] = pl.BlockSpec((bm, bn), lambda j, i: (r0 + i, j))
        out_shape[0] = jax.ShapeDtypeStruct((stack_depth * m, n), out_dtypes[0])
        if stacked is not None:
            aliases = {len(in_specs): 0}
            in_specs.append(pl.BlockSpec(memory_space=pl.ANY))
            stacked_in = (stacked,)
    outs = pl.pallas_call(
        functools.partial(_mm_kernel, epilogue=epilogue, n_out=len(out_dtypes), alpha=alpha,
                          w_rows=w_rows, shift=shift),
        grid=(n // bn, m // bm),
        in_specs=in_specs,
        out_specs=out_specs,
        out_shape=out_shape,
        input_output_aliases=aliases,
        scratch_shapes=[pltpu.VMEM(w16_shape, BF16)],
        compiler_params=pltpu.CompilerParams(
            dimension_semantics=("arbitrary", "arbitrary"), vmem_limit_bytes=V7X_VMEM_LIMIT),
        name=name,
    )(x, *([w] * len(w_specs)), *extra, *stacked_in)
    return outs


def _cum_kernel(lf_ref, c_ref, cts_ref, ctm_ref, *, n_chunks, chunk):
    def tri(n):
        r = lax.broadcasted_iota(jnp.int32, (n, n), 0)
        c = lax.broadcasted_iota(jnp.int32, (n, n), 1)
        return (c <= r).astype(BF16)

    c0 = _dot3(tri(LANES), lf_ref[0:LANES, :])
    c_ref[0:LANES, :] = c0
    ctm_ref[...] = c0.T[0:ctm_ref.shape[0], :]
    carry = c0[N_META - 1:N_META, :]
    tri_c = tri(chunk)
    for j in range(n_chunks):
        r0 = N_META + j * chunk
        cj = _dot3(tri_c, lf_ref[r0:r0 + chunk, :]) + carry
        c_ref[r0:r0 + chunk, :] = cj
        cts_ref[:, j * chunk:(j + 1) * chunk] = cj.T[0:cts_ref.shape[0], :]
        carry = cj[chunk - 1:chunk, :]


def _prompt_cumsum(logf_pad, n_heads, chunk):
    b, t, _ = logf_pad.shape
    n_chunks = (t - N_META) // chunk
    return pl.pallas_call(
        functools.partial(_cum_kernel, n_chunks=n_chunks, chunk=chunk),
        grid=(b,),
        in_specs=[pl.BlockSpec((None, t, LANES), lambda i: (i, 0, 0))],
        out_specs=[pl.BlockSpec((None, t, LANES), lambda i: (i, 0, 0)),
                   pl.BlockSpec((None, n_heads, t - N_META), lambda i: (i, 0, 0)),
                   pl.BlockSpec((None, n_heads, LANES), lambda i: (i, 0, 0))],
        out_shape=[jax.ShapeDtypeStruct((b, t, LANES), F32),
                   jax.ShapeDtypeStruct((b, n_heads, t - N_META), F32),
                   jax.ShapeDtypeStruct((b, n_heads, LANES), F32)],
        compiler_params=pltpu.CompilerParams(dimension_semantics=("arbitrary",)),
        name="prompt_cumsum",
    )(logf_pad)


LOG2E = math.log2(math.e)


def _causal_mask(n):
    r = lax.broadcasted_iota(jnp.int32, (n, n), 0)
    c = lax.broadcasted_iota(jnp.int32, (n, n), 1)
    return c <= r


def _mask_tail(s, mask):
    w = mask.shape[1]
    tail = jnp.where(mask, s[:, s.shape[1] - w:], -jnp.inf)
    if s.shape[1] == w:
        return tail
    return jnp.concatenate([s[:, :s.shape[1] - w], tail], axis=1)


def _prefix_softmax(s_head, s_tok, v_head, v_tok):
    m = jnp.max(s_head, axis=1, keepdims=True)
    if s_tok is not None:
        m = jnp.maximum(m, jnp.max(s_tok, axis=1, keepdims=True))
    p = jnp.exp2(s_head - m)
    l = jnp.sum(p, axis=1, keepdims=True)
    acc = jnp.dot(p.astype(BF16), v_head, preferred_element_type=F32)
    if s_tok is not None:
        p = jnp.exp2(s_tok - m)
        l = l + jnp.sum(p, axis=1, keepdims=True)
        acc = acc + jnp.dot(p.astype(BF16), v_tok, preferred_element_type=F32)
    return l, acc


def _fox_kernel(q_ref, k_ref, v_ref, z_ref, c_ref, cts_ref, ctm_ref, o_ref, *, scale, n_blk, blk):
    h = pl.program_id(1)
    diag = _causal_mask(blk)
    meta_cols = lax.broadcasted_iota(jnp.int32, (blk, LANES), 1) < N_META

    def attend(r0, n_rows, n_store, n_tok, head_mask):
        q = q_ref[r0:r0 + n_rows, :]
        cb = c_ref[r0:r0 + n_rows, :]
        lane = lax.broadcasted_iota(jnp.int32, cb.shape, 1)
        cq = jnp.sum(jnp.where(lane == h, cb, 0.0), axis=1, keepdims=True) * LOG2E

        def scores(k, ck):
            s = lax.dot_general(q, k, _NT, preferred_element_type=F32)
            return s * (scale * LOG2E) + (cq - ck * LOG2E)

        s_head = jnp.where(head_mask, scores(k_ref[0:LANES, :], ctm_ref[...]), -jnp.inf)
        s_tok = v_tok = None
        if n_tok:
            s_tok = _mask_tail(scores(k_ref[N_META:N_META + n_tok, :], cts_ref[:, 0:n_tok]), diag)
            v_tok = v_ref[N_META:N_META + n_tok, :]
        l, acc = _prefix_softmax(s_head, s_tok, v_ref[0:LANES, :], v_tok)
        z = z_ref[r0:r0 + n_rows, :]
        x = (acc / l) * (z * jax.nn.sigmoid(z))
        o_ref[r0:r0 + n_store, :] = x[0:n_store].astype(o_ref.dtype)

    attend(0, LANES, N_META, 0, _causal_mask(LANES))
    for i in range(n_blk):
        attend(N_META + i * blk, blk, blk, (i + 1) * blk, meta_cols)


def _fox_attention(q, k, v, zg, c, cts, ctm, *, n_heads, blk):
    b, t, _ = q.shape
    n_blk = (t - N_META) // blk
    hd = pl.BlockSpec((None, t, HEAD_DIM), lambda i, h: (i, 0, h))
    return pl.pallas_call(
        functools.partial(_fox_kernel, scale=HEAD_DIM ** -0.5, n_blk=n_blk, blk=blk),
        grid=(b, n_heads),
        in_specs=[hd, hd, hd, hd,
                  pl.BlockSpec((None, t, LANES), lambda i, h: (i, 0, 0)),
                  pl.BlockSpec((None, None, 1, t - N_META), lambda i, h: (i, h, 0, 0)),
                  pl.BlockSpec((None, None, 1, LANES), lambda i, h: (i, h, 0, 0))],
        out_specs=hd,
        out_shape=jax.ShapeDtypeStruct((b, t, n_heads * HEAD_DIM), BF16),
        compiler_params=pltpu.CompilerParams(dimension_semantics=("arbitrary", "arbitrary")),
        name="fox_attention",
    )(q, k, v, zg, c, cts.reshape(b, n_heads, 1, t - N_META), ctm.reshape(b, n_heads, 1, LANES))


def _diff_kernel(q_ref, k_ref, v_ref, z_ref, g_ref, lq1, lk1, lq2, lk2, o_ref, *,
                 scale, n_blk, blk, lam_init):
    lam = _diff_lambda(lq1[...], lk1[...], lq2[...], lk2[...], lam_init)
    diag = _causal_mask(blk)
    meta_cols = lax.broadcasted_iota(jnp.int32, (blk, LANES), 1) < N_META

    def attend(r0, n_rows, n_store, n_tok, head_mask):
        v_head = v_ref[0:LANES, :]
        v_tok = v_ref[N_META:N_META + n_tok, :] if n_tok else None
        o_maps = []
        for c in range(2):
            sl = slice(c * HEAD_DIM, (c + 1) * HEAD_DIM)
            q = q_ref[r0:r0 + n_rows, sl]

            def scores(k):
                return lax.dot_general(q, k, _NT, preferred_element_type=F32) * (scale * LOG2E)

            s_head = jnp.where(head_mask, scores(k_ref[0:LANES, sl]), -jnp.inf)
            s_tok = _mask_tail(scores(k_ref[N_META:N_META + n_tok, sl]), diag) if n_tok else None
            l, acc = _prefix_softmax(s_head, s_tok, v_head, v_tok)
            o_maps.append(acc / l)
        o = o_maps[0] - lam * o_maps[1]
        o = o * lax.rsqrt(jnp.mean(o * o, axis=1, keepdims=True) + EPS) * g_ref[...]
        o = o * (1.0 - lam_init)
        z = z_ref[r0:r0 + n_rows, :]
        x = o * (z * jax.nn.sigmoid(z))
        o_ref[r0:r0 + n_store, :] = x[0:n_store].astype(o_ref.dtype)

    attend(0, LANES, N_META, 0, _causal_mask(LANES))
    for i in range(n_blk):
        attend(N_META + i * blk, blk, blk, (i + 1) * blk, meta_cols)


def _diff_attention(q, k, v, zg, z_col0, g, lams, *, n_heads, blk, lam_init):
    b, t, _ = q.shape
    dv = 2 * HEAD_DIM
    n_blk = (t - N_META) // blk
    zoff = z_col0 // dv
    hd = pl.BlockSpec((None, t, dv), lambda i, h: (i, 0, h))
    vec = pl.BlockSpec((1, HEAD_DIM), lambda i, h: (0, 0))
    return pl.pallas_call(
        functools.partial(_diff_kernel, scale=HEAD_DIM ** -0.5, n_blk=n_blk, blk=blk,
                          lam_init=lam_init),
        grid=(b, n_heads),
        in_specs=[hd, hd, hd,
                  pl.BlockSpec((None, t, dv), lambda i, h: (i, 0, h + zoff)),
                  pl.BlockSpec((1, dv), lambda i, h: (0, 0)),
                  vec, vec, vec, vec],
        out_specs=hd,
        out_shape=jax.ShapeDtypeStruct((b, t, n_heads * dv), BF16),
        compiler_params=pltpu.CompilerParams(dimension_semantics=("arbitrary", "arbitrary")),
        name="diff_attention",
    )(q, k, v, zg, g, *lams)


def _gate_kernel(xa_ref, xb_ref, wa_ref, wb_ref, ga_ref, gb_ref, o_ref, wa16_ref, wb16_ref):
    br_a = jnp.dot(xa_ref[...], _resident_bf16(wa_ref, wa16_ref), preferred_element_type=F32)
    br_b = jnp.dot(xb_ref[...], _resident_bf16(wb_ref, wb16_ref), preferred_element_type=F32)
    m = jax.nn.sigmoid(ga_ref[...]) * br_a + jax.nn.sigmoid(gb_ref[...]) * br_b
    o_ref[...] = m.astype(o_ref.dtype)


def _gated_merge(xa, xb, w_oa, w_ob, layer, zg, ga_col0, gb_col0, *, bm, bn):
    m, k = xa.shape
    n = w_oa.shape[2]
    xs = pl.BlockSpec((bm, k), lambda j, i: (i, 0))
    ws = pl.BlockSpec((None, k, bn), lambda j, i: (layer, 0, j))
    ga_off, gb_off = ga_col0 // bn, gb_col0 // bn
    assert ga_col0 % bn == 0 and gb_col0 % bn == 0
    return pl.pallas_call(
        _gate_kernel,
        grid=(n // bn, m // bm),
        in_specs=[xs, xs, ws, ws,
                  pl.BlockSpec((bm, bn), lambda j, i: (i, j + ga_off)),
                  pl.BlockSpec((bm, bn), lambda j, i: (i, j + gb_off))],
        out_specs=pl.BlockSpec((bm, bn), lambda j, i: (i, j)),
        out_shape=jax.ShapeDtypeStruct((m, n), BF16),
        scratch_shapes=[pltpu.VMEM((k, bn), BF16), pltpu.VMEM((k, bn), BF16)],
        compiler_params=pltpu.CompilerParams(
            dimension_semantics=("arbitrary", "arbitrary"), vmem_limit_bytes=V7X_VMEM_LIMIT),
        name="gated_merge",
    )(xa, xb, w_oa, w_ob, zg, zg)


def _ln_kernel(u_ref, g_ref, b_ref, y_ref, yb_ref):
    u = u_ref[...]
    mu = jnp.mean(u, axis=1, keepdims=True)
    d = u - mu
    var = jnp.mean(d * d, axis=1, keepdims=True)
    y = d * lax.rsqrt(var + EPS) * g_ref[...] + b_ref[...]
    y_ref[...] = y
    yb_ref[...] = y.astype(yb_ref.dtype)


def _layer_norm(u, ln_g, ln_b, *, bm):
    m, n = u.shape
    row = pl.BlockSpec((bm, n), lambda i: (i, 0))
    vec = pl.BlockSpec((1, n), lambda i: (0, 0))
    return pl.pallas_call(
        _ln_kernel,
        grid=(m // bm,),
        in_specs=[row, vec, vec],
        out_specs=[row, row],
        out_shape=[jax.ShapeDtypeStruct((m, n), F32), jax.ShapeDtypeStruct((m, n), BF16)],
        compiler_params=pltpu.CompilerParams(dimension_semantics=("arbitrary",)),
        name="layer_norm",
    )(u, ln_g, ln_b)


def _out_ln(m_bf16, w_o, layer, h, ln_g, ln_b, *, bm, bm_ln, alpha):
    u, = _matmul(m_bf16, w_o, layer, [F32], bm=bm, bn=512, n=w_o.shape[2], epilogue="residual",
                 extra=(h,), alpha=alpha, name="out_proj")
    return _layer_norm(u, ln_g, ln_b, bm=bm_ln)


def _decode_bias_kernel(pt_ref, pool_ref, new_ref, o_ref, g_ref, *, n_pages, n_heads):
    b = pl.program_id(0)
    w = pool_ref.shape[1]

    def gather(j, carry):
        g_ref[pl.ds(j, 1), :] = pool_ref[pl.ds(pt_ref[b, j], 1), :]
        return carry

    lax.fori_loop(0, n_pages, gather, 0)

    r = lax.broadcasted_iota(jnp.int32, (LANES, LANES), 0)
    c = lax.broadcasted_iota(jnp.int32, (LANES, LANES), 1)
    same_head = (r % n_heads) == (c % n_heads)
    head_sum = same_head.astype(BF16)
    later_in_block = (same_head & (r > c)).astype(BF16)
    after = jnp.zeros((n_pages, LANES), F32)
    for kb in reversed(range(w // LANES)):
        sl = slice(kb * LANES, (kb + 1) * LANES)
        gk = g_ref[:, sl]
        o_ref[:, sl] = _dot3_left(gk, later_in_block) + _dot3_left(after, head_sum)
        after = after + gk
    total = _dot3_left(after, head_sum)
    pr = lax.broadcasted_iota(jnp.int32, (n_pages, n_pages), 0)
    pc = lax.broadcasted_iota(jnp.int32, (n_pages, n_pages), 1)
    rest = _dot3((pc > pr).astype(BF16), total) + new_ref[pl.ds(b, 1), :]
    for kb in range(w // LANES):
        sl = slice(kb * LANES, (kb + 1) * LANES)
        o_ref[:, sl] = o_ref[:, sl] + rest


def _decode_bias(page_table, pool_flat, layer, logf_new_pad, *, n_heads):
    db, n_pages = page_table.shape
    _, n_pool, w = pool_flat.shape
    return pl.pallas_call(
        functools.partial(_decode_bias_kernel, n_pages=n_pages, n_heads=n_heads),
        grid_spec=pltpu.PrefetchScalarGridSpec(
            num_scalar_prefetch=1,
            grid=(db,),
            in_specs=[pl.BlockSpec((None, n_pool, w), lambda b, pt: (layer, 0, 0)),
                      pl.BlockSpec(logf_new_pad.shape, lambda b, pt: (0, 0))],
            out_specs=pl.BlockSpec((None, n_pages, w), lambda b, pt: (b, 0, 0)),
            scratch_shapes=[pltpu.VMEM((n_pages, w), F32)]),
        out_shape=jax.ShapeDtypeStruct((db, n_pages, w), F32),
        compiler_params=pltpu.CompilerParams(
            dimension_semantics=("arbitrary",), vmem_limit_bytes=V7X_VMEM_LIMIT),
        name="decode_bias",
    )(page_table, pool_flat, logf_new_pad)


def _decode_kernel(pt_ref, *refs, scale, lam_init, n_heads_f, n_heads_d, ppb):
    fk_refs, fv_refs = refs[0:ppb], refs[ppb:2 * ppb]
    dk_refs, dv_refs = refs[2 * ppb:3 * ppb], refs[3 * ppb:4 * ppb]
    (bias_ref, qf_ref, knf_ref, vnf_ref, qd_ref, knd_ref, vnd_ref, zf_ref, zd_ref, g_ref,
     lq1, lk1, lq2, lk2, xa_ref, xb_ref, mf, lf, af, md, ld, ad) = refs[4 * ppb:]
    j = pl.program_id(1)
    hp = 2 * n_heads_d
    bias_row0 = (j * ppb) % 8

    def bf(x):
        return x.astype(BF16)

    @pl.when(j == 0)
    def _():
        qn = bf(qf_ref[...]).astype(F32)
        s_new = jnp.sum(qn * bf(knf_ref[...]).astype(F32), axis=1, keepdims=True) * scale
        mf[...] = s_new
        lf[...] = jnp.ones_like(lf)
        af[...] = bf(vnf_ref[...]).astype(F32)
        for c in range(2):
            qn = bf(qd_ref[c]).astype(F32)
            s_new = jnp.sum(qn * bf(knd_ref[c]).astype(F32), axis=1, keepdims=True) * scale
            md[c * hp:(c + 1) * hp, :] = s_new
            ad[c * hp:(c + 1) * hp, :] = bf(vnd_ref[...]).astype(F32)
        ld[...] = jnp.ones_like(ld)

    qf = bf(qf_ref[...])
    s = jnp.concatenate(
        [lax.dot_general(qf, bf(r[...]), _NT, preferred_element_type=F32) * scale
         + bias_ref[pl.ds(bias_row0 + p, 1), :] for p, r in enumerate(fk_refs)], axis=1)
    pw = fk_refs[0].shape[0]
    row = lax.broadcasted_iota(jnp.int32, s.shape, 0)
    col = lax.broadcasted_iota(jnp.int32, s.shape, 1)
    s = jnp.where((col % n_heads_f) == row, s, -jnp.inf)
    m_old = mf[...]
    m_new = jnp.maximum(m_old, jnp.max(s, axis=1, keepdims=True))
    alpha = jnp.exp(m_old - m_new)
    p = jnp.exp(s - m_new)
    mf[...] = m_new
    lf[...] = alpha * lf[...] + jnp.sum(p, axis=1, keepdims=True)
    pb = bf(p)
    af[...] = alpha * af[...] + sum(
        jnp.dot(pb[:, i * pw:(i + 1) * pw], bf(r[...]), preferred_element_type=F32)
        for i, r in enumerate(fv_refs))

    pw = dk_refs[0].shape[0] // 2
    qd = [bf(qd_ref[c]) for c in range(2)]
    s = jnp.concatenate(
        [jnp.concatenate(
            [lax.dot_general(qd[c], bf(r[pl.ds(c, pw, stride=2), :]), _NT,
                             preferred_element_type=F32) for c in range(2)], axis=0)
         for r in dk_refs], axis=1) * scale
    row = lax.broadcasted_iota(jnp.int32, s.shape, 0)
    col = lax.broadcasted_iota(jnp.int32, s.shape, 1)
    s = jnp.where((col % n_heads_d) == (row % hp), s, -jnp.inf)
    m_old = md[...]
    m_new = jnp.maximum(m_old, jnp.max(s, axis=1, keepdims=True))
    alpha = jnp.exp(m_old - m_new)
    p = jnp.exp(s - m_new)
    md[...] = m_new
    ld[...] = alpha * ld[...] + jnp.sum(p, axis=1, keepdims=True)
    pb = bf(p)
    ad[...] = alpha * ad[...] + sum(
        jnp.dot(pb[:, i * pw:(i + 1) * pw], bf(r[...]), preferred_element_type=F32)
        for i, r in enumerate(dv_refs))

    @pl.when(j == pl.num_programs(1) - 1)
    def _():
        z = zf_ref[...]
        xa_ref[...] = ((af[...] / lf[...]) * (z * jax.nn.sigmoid(z))).astype(xa_ref.dtype)
        lam = _diff_lambda(lq1[...], lk1[...], lq2[...], lk2[...], lam_init)
        o = ad[0:hp, :] / ld[0:hp, :] - lam * (ad[hp:2 * hp, :] / ld[hp:2 * hp, :])
        o = o * lax.rsqrt(jnp.mean(o * o, axis=1, keepdims=True) + EPS) * g_ref[...]
        o = o * (1.0 - lam_init)
        z = zd_ref[...]
        xb_ref[...] = (o * (z * jax.nn.sigmoid(z))).astype(xb_ref.dtype)


def _decode_attention(page_table, fk, fv, dk, dv, bias, qf, knf, vnf, qd, knd, vnd, zf, zd,
                      g, lams, *, layer, n_pool, lam_init, ppb):
    db, n_pages = page_table.shape
    n_heads_f = qf.shape[1]
    hp = qd.shape[2]
    n_heads_d = hp // 2
    dvw = 2 * HEAD_DIM
    base = layer * n_pool
    assert 8 % ppb == 0 and n_pages % ppb == 0

    def pages(a):
        return [pl.BlockSpec((None,) + a.shape[1:],
                             lambda b, j, pt, p=p: (base + pt[b, j * ppb + p], 0, 0))
                for p in range(ppb)]

    def per_b(shape):
        nd = len(shape)
        return pl.BlockSpec((None,) + shape, lambda b, j, pt: (b,) + (0,) * nd)

    vec = pl.BlockSpec((1, HEAD_DIM), lambda b, j, pt: (0, 0))
    return pl.pallas_call(
        functools.partial(_decode_kernel, scale=HEAD_DIM ** -0.5, lam_init=lam_init,
                          n_heads_f=n_heads_f, n_heads_d=n_heads_d, ppb=ppb),
        grid_spec=pltpu.PrefetchScalarGridSpec(
            num_scalar_prefetch=1,
            grid=(db, n_pages // ppb),
            in_specs=pages(fk) + pages(fv) + pages(dk) + pages(dv) + [
                      pl.BlockSpec((None, 8, bias.shape[2]), lambda b, j, pt: (b, (j * ppb) // 8, 0)),
                      per_b(qf.shape[1:]), per_b(knf.shape[1:]), per_b(vnf.shape[1:]),
                      per_b(qd.shape[1:]), per_b(knd.shape[1:]), per_b(vnd.shape[1:]),
                      per_b(zf.shape[1:]), per_b(zd.shape[1:]),
                      pl.BlockSpec((1, dvw), lambda b, j, pt: (0, 0)),
                      vec, vec, vec, vec],
            out_specs=[per_b((n_heads_f, HEAD_DIM)), per_b((hp, dvw))],
            scratch_shapes=[pltpu.VMEM((n_heads_f, 1), F32), pltpu.VMEM((n_heads_f, 1), F32),
                            pltpu.VMEM((n_heads_f, HEAD_DIM), F32),
                            pltpu.VMEM((2 * hp, 1), F32), pltpu.VMEM((2 * hp, 1), F32),
                            pltpu.VMEM((2 * hp, dvw), F32)]),
        out_shape=[jax.ShapeDtypeStruct((db, n_heads_f, HEAD_DIM), BF16),
                   jax.ShapeDtypeStruct((db, hp, dvw), BF16)],
        compiler_params=pltpu.CompilerParams(
            dimension_semantics=("arbitrary", "arbitrary"), vmem_limit_bytes=V7X_VMEM_LIMIT),
        name="decode_attention",
    )(page_table, *([fk] * ppb + [fv] * ppb + [dk] * ppb + [dv] * ppb),
      bias, qf, knf, vnf, qd, knd, vnd, zf, zd, g, *lams)


def _rope_tables(pos):
    half = HEAD_DIM // 2
    inv = 1.0 / (ROPE_THETA ** (jnp.arange(half, dtype=F32) * 2.0 / HEAD_DIM))
    ang = pos.astype(F32)[:, None] * inv[None, :]
    cos, sin = jnp.cos(ang), jnp.sin(ang)
    return jnp.concatenate([cos, cos], axis=1), jnp.concatenate([-sin, sin], axis=1)


def _in_project(x, w_in_t, layer, b_f, cos2, sin2, *, bm, fw, dw, d, stack_depth=0, stacked=None):
    n_heads_f = b_f.shape[0]
    mm = functools.partial(_matmul, x, w_in_t, layer, bm=bm, bn=512, w_rows=True)
    rope = dict(epilogue="rope", extra=(cos2, sin2))
    tail0 = 4 * fw + n_heads_f

    def kv(name, **kw):
        prev = None if stacked is None else stacked[name]
        return mm([F32, BF16], n=kw.pop("n"), name="proj_" + name, stack_depth=stack_depth,
                  stacked=prev, **kw)

    qa, = mm([BF16], w0=0, n=fw, name="proj_qa")
    ka, ka16 = kv("ka", w0=fw, n=fw)
    va, va16 = kv("va", w0=2 * fw, n=fw)
    za, = mm([F32], w0=3 * fw, n=fw, name="proj_za")
    b_pad = jnp.pad(b_f.astype(F32), (0, LANES - n_heads_f)).reshape(1, LANES)
    logf_pad, = _matmul(x, w_in_t, layer, [F32], bm=bm, bn=LANES, n=LANES, w0=4 * fw, w_rows=True,
                        epilogue="logsig", extra=(b_pad,), name="proj_fa")
    qb, = mm([BF16], w0=tail0, n=dw, name="proj_qb", **rope)
    kb, kb16 = kv("kb", w0=tail0 + dw, n=dw, **rope)
    vb, vb16 = kv("vb", w0=tail0 + 2 * dw, n=dw)
    zgb, = mm([F32], w0=tail0 + 3 * dw, n=dw + 2 * d, name="proj_gates")
    return dict(qa=qa, ka=ka, ka16=ka16, va=va, va16=va16, za=za, logf_pad=logf_pad, qb=qb, kb=kb,
                kb16=kb16, vb=vb, vb16=vb16, zgb=zgb)


def kernel(x_prompt, x_sample, cache_fox_k, cache_fox_v, cache_fox_logf, cache_diff_k, cache_diff_v,
           page_table, meta_tokens, w_in, b_forget, lambda_q1, lambda_k1, lambda_q2, lambda_k2,
           diff_norm_g, w_out_fox, w_out_diff, w_out, ln_g, ln_b):
    nb, seq, d = x_prompt.shape
    depth = w_in.shape[0]
    t = seq + N_META
    n_heads_f = b_forget.shape[1]
    fw = n_heads_f * HEAD_DIM
    dw = w_out_diff.shape[1]
    n_heads_d = dw // (2 * HEAD_DIM)
    db, n_new, _ = x_sample.shape
    n_pool, page = cache_fox_k.shape[1], cache_fox_k.shape[2]
    n_pages = page_table.shape[1]
    past = n_pages * page
    assert n_new == 1 and db <= ROW_PAD and n_pages % 8 == 0
    alpha = (2 * depth) ** 0.25
    blk = 256
    bm = t // 3
    bm_ln = max(r for r in range(ROW_PAD, 193, ROW_PAD) if (nb * t) % r == 0)
    assert t % 3 == 0 and bm % ROW_PAD == 0 and (t - N_META) % blk == 0

    cos_p, sin_p = _rope_tables(jnp.arange(t, dtype=jnp.int32))
    cos_s, sin_s = _rope_tables(jnp.full((ROW_PAD,), past, dtype=jnp.int32))

    meta = jnp.broadcast_to(meta_tokens.astype(x_prompt.dtype)[None], (nb, N_META, d))
    h = jnp.concatenate([meta, x_prompt], axis=1).reshape(nb * t, d)
    h16 = h.astype(BF16)
    hs = jnp.pad(x_sample.reshape(db, d), ((0, ROW_PAD - db), (0, 0)))
    hs16 = hs.astype(BF16)

    fk_pool = cache_fox_k.reshape(depth * n_pool, page * n_heads_f, HEAD_DIM)
    fv_pool = cache_fox_v.reshape(depth * n_pool, page * n_heads_f, HEAD_DIM)
    dk_pool = cache_diff_k.reshape(depth * n_pool, page * n_heads_d * 2, HEAD_DIM)
    dv_pool = cache_diff_v.reshape(depth * n_pool, page * n_heads_d, 2 * HEAD_DIM)
    lf_pool = cache_fox_logf.astype(F32).reshape(depth, n_pool, page * n_heads_f)

    w_in_t = jnp.swapaxes(w_in, 1, 2)

    leaves_p, lf_p = None, []
    outs_s = {k: [] for k in ("ka", "va", "lf", "kb", "vb")}
    row = lambda a, l: a[l].astype(F32).reshape(1, -1)

    for l in range(depth):
        lam_init = _lambda_init(l)
        lams = (row(lambda_q1, l), row(lambda_k1, l), row(lambda_q2, l), row(lambda_k2, l))
        g = row(diff_norm_g, l)
        lng, lnb = row(ln_g, l), row(ln_b, l)
        proj = functools.partial(_in_project, w_in_t=w_in_t, layer=l, b_f=b_forget[l], fw=fw, dw=dw, d=d)
        merge = functools.partial(_gated_merge, w_oa=w_out_fox, w_ob=w_out_diff, layer=l,
                                  ga_col0=dw, gb_col0=dw + d, bn=512)

        p = proj(h16, cos2=cos_p, sin2=sin_p, bm=bm, stack_depth=depth, stacked=leaves_p)
        leaves_p = {k: p[k] for k in ("ka", "va", "kb", "vb")}
        r3 = lambda a: a.reshape(nb, t, -1)
        c, cts, ctm = _prompt_cumsum(r3(p["logf_pad"]), n_heads_f, blk)
        xa = _fox_attention(r3(p["qa"]), r3(p["ka16"]), r3(p["va16"]), r3(p["za"]), c, cts, ctm,
                            n_heads=n_heads_f, blk=blk)
        xb = _diff_attention(r3(p["qb"]), r3(p["kb16"]), r3(p["vb16"]), r3(p["zgb"]), 0, g, lams,
                             n_heads=n_heads_d, blk=blk, lam_init=lam_init)
        m = merge(xa.reshape(nb * t, fw), xb.reshape(nb * t, dw), zg=p["zgb"], bm=bm)
        h, h16 = _out_ln(m, w_out, l, h, lng, lnb, bm=bm, bm_ln=bm_ln, alpha=alpha)
        lf_p.append(p["logf_pad"][:, :n_heads_f].reshape(nb, t, n_heads_f))

        s = proj(hs16, cos2=cos_s, sin2=sin_s, bm=ROW_PAD)
        lf_new = jnp.tile(s["logf_pad"][:, :n_heads_f], (1, LANES // n_heads_f))
        bias = _decode_bias(page_table, lf_pool, l, lf_new, n_heads=n_heads_f)
        hp = 2 * n_heads_d

        def maps(a):
            a = a.reshape(ROW_PAD, n_heads_d, 2, HEAD_DIM).transpose(0, 2, 1, 3)
            return jnp.pad(a, ((0, 0), (0, 0), (0, hp - n_heads_d), (0, 0)))

        def vrows(a):
            a = a.reshape(ROW_PAD, n_heads_d, 2 * HEAD_DIM)
            return jnp.pad(a, ((0, 0), (0, hp - n_heads_d), (0, 0)))

        xa_s, xb_s = _decode_attention(
            page_table, fk_pool, fv_pool, dk_pool, dv_pool, bias,
            s["qa"].astype(F32).reshape(ROW_PAD, n_heads_f, HEAD_DIM),
            s["ka"].reshape(ROW_PAD, n_heads_f, HEAD_DIM),
            s["va"].reshape(ROW_PAD, n_heads_f, HEAD_DIM),
            maps(s["qb"].astype(F32)), maps(s["kb"]), vrows(s["vb"]),
            s["za"].reshape(ROW_PAD, n_heads_f, HEAD_DIM), vrows(s["zgb"][:, :dw]),
            g, lams, layer=l, n_pool=n_pool, lam_init=lam_init, ppb=4)
        xa_s = jnp.pad(xa_s.reshape(db, fw), ((0, ROW_PAD - db), (0, 0)))
        xb_s = jnp.pad(xb_s[:, :n_heads_d].reshape(db, dw), ((0, ROW_PAD - db), (0, 0)))
        ms = merge(xa_s, xb_s, zg=s["zgb"], bm=ROW_PAD)
        hs, hs16 = _out_ln(ms, w_out, l, hs, lng, lnb, bm=ROW_PAD, bm_ln=ROW_PAD, alpha=alpha)
        outs_s["ka"].append(s["ka"][:db].reshape(db, 1, n_heads_f, HEAD_DIM))
        outs_s["va"].append(s["va"][:db].reshape(db, 1, n_heads_f, HEAD_DIM))
        outs_s["lf"].append(s["logf_pad"][:db, :n_heads_f].reshape(db, 1, n_heads_f))
        outs_s["kb"].append(s["kb"][:db].reshape(db, 1, n_heads_d, 2, HEAD_DIM))
        outs_s["vb"].append(s["vb"][:db].reshape(db, 1, n_heads_d, 2 * HEAD_DIM))

    y_prompt = h.reshape(nb, t, d)[:, N_META:]
    y_sample = hs[:db].reshape(db, 1, d)
    st = jnp.stack
    return (y_prompt, y_sample,
            leaves_p["ka"].reshape(depth, nb, t, n_heads_f, HEAD_DIM),
            leaves_p["va"].reshape(depth, nb, t, n_heads_f, HEAD_DIM),
            st(lf_p),
            leaves_p["kb"].reshape(depth, nb, t, n_heads_d, 2, HEAD_DIM),
            leaves_p["vb"].reshape(depth, nb, t, n_heads_d, 2 * HEAD_DIM),
            st(outs_s["ka"]), st(outs_s["va"]), st(outs_s["lf"]), st(outs_s["kb"]), st(outs_s["vb"]))
```

```python
import functools
import math

import jax
import jax.numpy as jnp
from jax import lax
from jax.experimental import pallas as pl
from jax.experimental.pallas import tpu as pltpu

F32 = jnp.float32
BF16 = jnp.bfloat16

HEAD_DIM = 128
N_META = 16
ROPE_THETA = 10000.0
EPS = 1e-5
LANES = 128
ROW_PAD = 16
V7X_VMEM_LIMIT = 56 * 1024 * 1024

_NT = (((1,), (1,)), ((), ()))


def _lambda_init(layer):
    return 0.8 - 0.6 * math.exp(-0.3 * layer)


def _split3(x):
    hi = x.astype(BF16)
    r1 = x - hi.astype(F32)
    mid = r1.astype(BF16)
    lo = (r1 - mid.astype(F32)).astype(BF16)
    return hi, mid, lo


def _dot3(a_bf16, x_f32):
    return sum(jnp.dot(a_bf16, p, preferred_element_type=F32) for p in _split3(x_f32))


def _dot3_left(x_f32, a_bf16):
    return sum(jnp.dot(p, a_bf16, preferred_element_type=F32) for p in _split3(x_f32))


def _diff_lambda(lq1, lk1, lq2, lk2, lam_init):
    e1 = jnp.exp(jnp.sum(lq1 * lk1, axis=1, keepdims=True))
    e2 = jnp.exp(jnp.sum(lq2 * lk2, axis=1, keepdims=True))
    return e1 - e2 + lam_init


def _mm_kernel(x_ref, w_ref, *rest, epilogue, n_out, alpha, w_rows, shift):
    if shift:
        w_next_ref, w_tile_ref = rest[0], rest[-1]
        rest = rest[1:-1]
        bn = w_tile_ref.shape[0]

        @pl.when(pl.program_id(1) == 0)
        def _():
            w_tile_ref[0:bn - shift, :] = w_ref[shift:bn, :]
            w_tile_ref[bn - shift:bn, :] = w_next_ref[...]

        w = w_tile_ref[...]
    else:
        w = w_ref[...]
    outs = rest[len(rest) - n_out:]
    extra = rest[:len(rest) - n_out]
    if w_rows:
        acc = lax.dot_general(x_ref[...], w, _NT, preferred_element_type=F32)
    else:
        acc = jnp.dot(x_ref[...], w, preferred_element_type=F32)
    if epilogue == "residual":
        acc = alpha * extra[0][...] + acc
    if epilogue == "logsig":
        z = acc + extra[0][...]
        acc = jnp.minimum(z, 0.0) - jnp.log1p(jnp.exp(-jnp.abs(z)))
    if epilogue == "rope":
        cos = extra[0][...]
        sin = extra[1][...]
        for g in range(acc.shape[1] // HEAD_DIM):
            sl = slice(g * HEAD_DIM, (g + 1) * HEAD_DIM)
            xg = acc[:, sl]
            r = xg * cos + pltpu.roll(xg, HEAD_DIM // 2, 1) * sin
            for o in outs:
                o[:, sl] = r.astype(o.dtype)
    else:
        for o in outs:
            o[...] = acc.astype(o.dtype)


def _matmul(x, w, layer, out_dtypes, *, bm, bn, n, w0=0, w_rows=False, epilogue="none", extra=(),
            alpha=None, name="mm", stack_depth=0, stacked=None):
    m, k = x.shape
    shift = w0 % bn
    assert m % bm == 0 and n % bn == 0 and shift in (0, ROW_PAD) and (w_rows or not shift)
    b0 = w0 // bn
    if w_rows:
        w_specs = [pl.BlockSpec((None, bn, k), lambda j, i: (layer, b0 + j, 0))]
        if shift:
            per = bn // shift
            w_specs.append(pl.BlockSpec((None, shift, k), lambda j, i: (layer, (b0 + j + 1) * per, 0)))
    else:
        w_specs = [pl.BlockSpec((None, k, bn), lambda j, i: (layer, 0, b0 + j))]
    in_specs = [pl.BlockSpec((bm, k), lambda j, i: (i, 0))] + w_specs
    if epilogue == "logsig":
        in_specs.append(pl.BlockSpec((1, bn), lambda j, i: (0, j)))
    elif epilogue == "rope":
        nt = extra[0].shape[0] // bm
        in_specs += [pl.BlockSpec((bm, HEAD_DIM), lambda j, i: (i % nt, 0))] * 2
    elif epilogue == "residual":
        in_specs.append(pl.BlockSpec((bm, bn), lambda j, i: (i, j)))
    out_specs = [pl.BlockSpec((bm, bn), lambda j, i: (i, j)) for _ in out_dtypes]
    out_shape = [jax.ShapeDtypeStruct((m, n), d) for d in out_dtypes]
    aliases, stacked_in = {}, ()
    if stack_depth:
        r0 = layer * (m // bm)
        out_specs[0] = pl.BlockSpec((bm, bn), lambda j, i: (r0 + i, j))
        out_shape[0] = jax.ShapeDtypeStruct((stack_depth * m, n), out_dtypes[0])
        if stacked is not None:
            aliases = {len(in_specs): 0}
            in_specs.append(pl.BlockSpec(memory_space=pl.ANY))
            stacked_in = (stacked,)
    outs = pl.pallas_call(
        functools.partial(_mm_kernel, epilogue=epilogue, n_out=len(out_dtypes), alpha=alpha,
                          w_rows=w_rows, shift=shift),
        grid=(n // bn, m // bm),
        in_specs=in_specs,
        out_specs=out_specs,
        out_shape=out_shape,
        input_output_aliases=aliases,
        scratch_shapes=[pltpu.VMEM((bn, k), BF16)] if shift else [],
        compiler_params=pltpu.CompilerParams(
            dimension_semantics=("arbitrary", "arbitrary"), vmem_limit_bytes=V7X_VMEM_LIMIT),
        name=name,
    )(x, *([w] * len(w_specs)), *extra, *stacked_in)
    return outs


def _cum_kernel(lf_ref, c_ref, cts_ref, ctm_ref, *, n_chunks, chunk):
    def tri(n):
        r = lax.broadcasted_iota(jnp.int32, (n, n), 0)
        c = lax.broadcasted_iota(jnp.int32, (n, n), 1)
        return (c <= r).astype(BF16)

    c0 = _dot3(tri(LANES), lf_ref[0:LANES, :])
    c_ref[0:LANES, :] = c0
    ctm_ref[...] = c0.T[0:ctm_ref.shape[0], :]
    carry = c0[N_META - 1:N_META, :]
    tri_c = tri(chunk)
    for j in range(n_chunks):
        r0 = N_META + j * chunk
        cj = _dot3(tri_c, lf_ref[r0:r0 + chunk, :]) + carry
        c_ref[r0:r0 + chunk, :] = cj
        cts_ref[:, j * chunk:(j + 1) * chunk] = cj.T[0:cts_ref.shape[0], :]
        carry = cj[chunk - 1:chunk, :]


def _prompt_cumsum(logf_pad, n_heads, chunk):
    b, t, _ = logf_pad.shape
    n_chunks = (t - N_META) // chunk
    return pl.pallas_call(
        functools.partial(_cum_kernel, n_chunks=n_chunks, chunk=chunk),
        grid=(b,),
        in_specs=[pl.BlockSpec((None, t, LANES), lambda i: (i, 0, 0))],
        out_specs=[pl.BlockSpec((None, t, LANES), lambda i: (i, 0, 0)),
                   pl.BlockSpec((None, n_heads, t - N_META), lambda i: (i, 0, 0)),
                   pl.BlockSpec((None, n_heads, LANES), lambda i: (i, 0, 0))],
        out_shape=[jax.ShapeDtypeStruct((b, t, LANES), F32),
                   jax.ShapeDtypeStruct((b, n_heads, t - N_META), F32),
                   jax.ShapeDtypeStruct((b, n_heads, LANES), F32)],
        compiler_params=pltpu.CompilerParams(dimension_semantics=("arbitrary",)),
        name="prompt_cumsum",
    )(logf_pad)


LOG2E = math.log2(math.e)


def _causal_mask(n):
    r = lax.broadcasted_iota(jnp.int32, (n, n), 0)
    c = lax.broadcasted_iota(jnp.int32, (n, n), 1)
    return c <= r


def _mask_tail(s, mask):
    w = mask.shape[1]
    tail = jnp.where(mask, s[:, s.shape[1] - w:], -jnp.inf)
    if s.shape[1] == w:
        return tail
    return jnp.concatenate([s[:, :s.shape[1] - w], tail], axis=1)


def _prefix_softmax(s_head, s_tok, v_head, v_tok):
    m = jnp.max(s_head, axis=1, keepdims=True)
    if s_tok is not None:
        m = jnp.maximum(m, jnp.max(s_tok, axis=1, keepdims=True))
    p = jnp.exp2(s_head - m)
    l = jnp.sum(p, axis=1, keepdims=True)
    acc = jnp.dot(p.astype(BF16), v_head, preferred_element_type=F32)
    if s_tok is not None:
        p = jnp.exp2(s_tok - m)
        l = l + jnp.sum(p, axis=1, keepdims=True)
        acc = acc + jnp.dot(p.astype(BF16), v_tok, preferred_element_type=F32)
    return l, acc


def _fox_kernel(q_ref, k_ref, v_ref, z_ref, c_ref, cts_ref, ctm_ref, o_ref, *, scale, n_blk, blk):
    h = pl.program_id(1)
    diag = _causal_mask(blk)
    meta_cols = lax.broadcasted_iota(jnp.int32, (blk, LANES), 1) < N_META

    def attend(r0, n_rows, n_store, n_tok, head_mask):
        q = q_ref[r0:r0 + n_rows, :]
        cb = c_ref[r0:r0 + n_rows, :]
        lane = lax.broadcasted_iota(jnp.int32, cb.shape, 1)
        cq = jnp.sum(jnp.where(lane == h, cb, 0.0), axis=1, keepdims=True) * LOG2E

        def scores(k, ck):
            s = lax.dot_general(q, k, _NT, preferred_element_type=F32)
            return s * (scale * LOG2E) + (cq - ck * LOG2E)

        s_head = jnp.where(head_mask, scores(k_ref[0:LANES, :], ctm_ref[...]), -jnp.inf)
        s_tok = v_tok = None
        if n_tok:
            s_tok = _mask_tail(scores(k_ref[N_META:N_META + n_tok, :], cts_ref[:, 0:n_tok]), diag)
            v_tok = v_ref[N_META:N_META + n_tok, :]
        l, acc = _prefix_softmax(s_head, s_tok, v_ref[0:LANES, :], v_tok)
        z = z_ref[r0:r0 + n_rows, :]
        x = (acc / l) * (z * jax.nn.sigmoid(z))
        o_ref[r0:r0 + n_store, :] = x[0:n_store].astype(o_ref.dtype)

    attend(0, LANES, N_META, 0, _causal_mask(LANES))
    for i in range(n_blk):
        attend(N_META + i * blk, blk, blk, (i + 1) * blk, meta_cols)


def _fox_attention(q, k, v, zg, c, cts, ctm, *, n_heads, blk):
    b, t, _ = q.shape
    n_blk = (t - N_META) // blk
    hd = pl.BlockSpec((None, t, HEAD_DIM), lambda i, h: (i, 0, h))
    return pl.pallas_call(
        functools.partial(_fox_kernel, scale=HEAD_DIM ** -0.5, n_blk=n_blk, blk=blk),
        grid=(b, n_heads),
        in_specs=[hd, hd, hd, hd,
                  pl.BlockSpec((None, t, LANES), lambda i, h: (i, 0, 0)),
                  pl.BlockSpec((None, None, 1, t - N_META), lambda i, h: (i, h, 0, 0)),
                  pl.BlockSpec((None, None, 1, LANES), lambda i, h: (i, h, 0, 0))],
        out_specs=hd,
        out_shape=jax.ShapeDtypeStruct((b, t, n_heads * HEAD_DIM), BF16),
        compiler_params=pltpu.CompilerParams(dimension_semantics=("arbitrary", "arbitrary")),
        name="fox_attention",
    )(q, k, v, zg, c, cts.reshape(b, n_heads, 1, t - N_META), ctm.reshape(b, n_heads, 1, LANES))


def _diff_kernel(q_ref, k_ref, v_ref, z_ref, g_ref, lq1, lk1, lq2, lk2, o_ref, *,
                 scale, n_blk, blk, lam_init):
    lam = _diff_lambda(lq1[...], lk1[...], lq2[...], lk2[...], lam_init)
    diag = _causal_mask(blk)
    meta_cols = lax.broadcasted_iota(jnp.int32, (blk, LANES), 1) < N_META

    def attend(r0, n_rows, n_store, n_tok, head_mask):
        v_head = v_ref[0:LANES, :]
        v_tok = v_ref[N_META:N_META + n_tok, :] if n_tok else None
        o_maps = []
        for c in range(2):
            sl = slice(c * HEAD_DIM, (c + 1) * HEAD_DIM)
            q = q_ref[r0:r0 + n_rows, sl]

            def scores(k):
                return lax.dot_general(q, k, _NT, preferred_element_type=F32) * (scale * LOG2E)

            s_head = jnp.where(head_mask, scores(k_ref[0:LANES, sl]), -jnp.inf)
            s_tok = _mask_tail(scores(k_ref[N_META:N_META + n_tok, sl]), diag) if n_tok else None
            l, acc = _prefix_softmax(s_head, s_tok, v_head, v_tok)
            o_maps.append(acc / l)
        o = o_maps[0] - lam * o_maps[1]
        o = o * lax.rsqrt(jnp.mean(o * o, axis=1, keepdims=True) + EPS) * g_ref[...]
        o = o * (1.0 - lam_init)
        z = z_ref[r0:r0 + n_rows, :]
        x = o * (z * jax.nn.sigmoid(z))
        o_ref[r0:r0 + n_store, :] = x[0:n_store].astype(o_ref.dtype)

    attend(0, LANES, N_META, 0, _causal_mask(LANES))
    for i in range(n_blk):
        attend(N_META + i * blk, blk, blk, (i + 1) * blk, meta_cols)


def _diff_attention(q, k, v, zg, z_col0, g, lams, *, n_heads, blk, lam_init):
    b, t, _ = q.shape
    dv = 2 * HEAD_DIM
    n_blk = (t - N_META) // blk
    zoff = z_col0 // dv
    hd = pl.BlockSpec((None, t, dv), lambda i, h: (i, 0, h))
    vec = pl.BlockSpec((1, HEAD_DIM), lambda i, h: (0, 0))
    return pl.pallas_call(
        functools.partial(_diff_kernel, scale=HEAD_DIM ** -0.5, n_blk=n_blk, blk=blk,
                          lam_init=lam_init),
        grid=(b, n_heads),
        in_specs=[hd, hd, hd,
                  pl.BlockSpec((None, t, dv), lambda i, h: (i, 0, h + zoff)),
                  pl.BlockSpec((1, dv), lambda i, h: (0, 0)),
                  vec, vec, vec, vec],
        out_specs=hd,
        out_shape=jax.ShapeDtypeStruct((b, t, n_heads * dv), BF16),
        compiler_params=pltpu.CompilerParams(dimension_semantics=("arbitrary", "arbitrary")),
        name="diff_attention",
    )(q, k, v, zg, g, *lams)


def _gate_kernel(xa_ref, xb_ref, wa_ref, wb_ref, ga_ref, gb_ref, o_ref):
    br_a = jnp.dot(xa_ref[...], wa_ref[...], preferred_element_type=F32)
    br_b = jnp.dot(xb_ref[...], wb_ref[...], preferred_element_type=F32)
    m = jax.nn.sigmoid(ga_ref[...]) * br_a + jax.nn.sigmoid(gb_ref[...]) * br_b
    o_ref[...] = m.astype(o_ref.dtype)


def _gated_merge(xa, xb, w_oa, w_ob, layer, zg, ga_col0, gb_col0, *, bm, bn):
    m, k = xa.shape
    n = w_oa.shape[2]
    xs = pl.BlockSpec((bm, k), lambda j, i: (i, 0))
    ws = pl.BlockSpec((None, k, bn), lambda j, i: (layer, 0, j))
    ga_off, gb_off = ga_col0 // bn, gb_col0 // bn
    assert ga_col0 % bn == 0 and gb_col0 % bn == 0
    return pl.pallas_call(
        _gate_kernel,
        grid=(n // bn, m // bm),
        in_specs=[xs, xs, ws, ws,
                  pl.BlockSpec((bm, bn), lambda j, i: (i, j + ga_off)),
                  pl.BlockSpec((bm, bn), lambda j, i: (i, j + gb_off))],
        out_specs=pl.BlockSpec((bm, bn), lambda j, i: (i, j)),
        out_shape=jax.ShapeDtypeStruct((m, n), BF16),
        compiler_params=pltpu.CompilerParams(
            dimension_semantics=("arbitrary", "arbitrary"), vmem_limit_bytes=V7X_VMEM_LIMIT),
        name="gated_merge",
    )(xa, xb, w_oa, w_ob, zg, zg)


def _ln_kernel(u_ref, g_ref, b_ref, y_ref, yb_ref):
    u = u_ref[...]
    mu = jnp.mean(u, axis=1, keepdims=True)
    d = u - mu
    var = jnp.mean(d * d, axis=1, keepdims=True)
    y = d * lax.rsqrt(var + EPS) * g_ref[...] + b_ref[...]
    y_ref[...] = y
    yb_ref[...] = y.astype(yb_ref.dtype)


def _layer_norm(u, ln_g, ln_b, *, bm):
    m, n = u.shape
    row = pl.BlockSpec((bm, n), lambda i: (i, 0))
    vec = pl.BlockSpec((1, n), lambda i: (0, 0))
    return pl.pallas_call(
        _ln_kernel,
        grid=(m // bm,),
        in_specs=[row, vec, vec],
        out_specs=[row, row],
        out_shape=[jax.ShapeDtypeStruct((m, n), F32), jax.ShapeDtypeStruct((m, n), BF16)],
        compiler_params=pltpu.CompilerParams(dimension_semantics=("arbitrary",)),
        name="layer_norm",
    )(u, ln_g, ln_b)


def _out_ln(m_bf16, w_o, layer, h, ln_g, ln_b, *, bm, bm_ln, alpha):
    u, = _matmul(m_bf16, w_o, layer, [F32], bm=bm, bn=1024, n=w_o.shape[2], epilogue="residual",
                 extra=(h,), alpha=alpha, name="out_proj")
    return _layer_norm(u, ln_g, ln_b, bm=bm_ln)


def _decode_bias_kernel(pt_ref, pool_ref, new_ref, o_ref, g_ref, *, n_pages, n_heads):
    b = pl.program_id(0)
    w = pool_ref.shape[1]

    def gather(j, carry):
        g_ref[pl.ds(j, 1), :] = pool_ref[pl.ds(pt_ref[b, j], 1), :]
        return carry

    lax.fori_loop(0, n_pages, gather, 0)

    r = lax.broadcasted_iota(jnp.int32, (LANES, LANES), 0)
    c = lax.broadcasted_iota(jnp.int32, (LANES, LANES), 1)
    same_head = (r % n_heads) == (c % n_heads)
    head_sum = same_head.astype(BF16)
    later_in_block = (same_head & (r > c)).astype(BF16)
    after = jnp.zeros((n_pages, LANES), F32)
    for kb in reversed(range(w // LANES)):
        sl = slice(kb * LANES, (kb + 1) * LANES)
        gk = g_ref[:, sl]
        o_ref[:, sl] = _dot3_left(gk, later_in_block) + _dot3_left(after, head_sum)
        after = after + gk
    total = _dot3_left(after, head_sum)
    pr = lax.broadcasted_iota(jnp.int32, (n_pages, n_pages), 0)
    pc = lax.broadcasted_iota(jnp.int32, (n_pages, n_pages), 1)
    rest = _dot3((pc > pr).astype(BF16), total) + new_ref[pl.ds(b, 1), :]
    for kb in range(w // LANES):
        sl = slice(kb * LANES, (kb + 1) * LANES)
        o_ref[:, sl] = o_ref[:, sl] + rest


def _decode_bias(page_table, pool_flat, layer, logf_new_pad, *, n_heads):
    db, n_pages = page_table.shape
    _, n_pool, w = pool_flat.shape
    return pl.pallas_call(
        functools.partial(_decode_bias_kernel, n_pages=n_pages, n_heads=n_heads),
        grid_spec=pltpu.PrefetchScalarGridSpec(
            num_scalar_prefetch=1,
            grid=(db,),
            in_specs=[pl.BlockSpec((None, n_pool, w), lambda b, pt: (layer, 0, 0)),
                      pl.BlockSpec(logf_new_pad.shape, lambda b, pt: (0, 0))],
            out_specs=pl.BlockSpec((None, n_pages, w), lambda b, pt: (b, 0, 0)),
            scratch_shapes=[pltpu.VMEM((n_pages, w), F32)]),
        out_shape=jax.ShapeDtypeStruct((db, n_pages, w), F32),
        compiler_params=pltpu.CompilerParams(
            dimension_semantics=("arbitrary",), vmem_limit_bytes=V7X_VMEM_LIMIT),
        name="decode_bias",
    )(page_table, pool_flat, logf_new_pad)


def _decode_kernel(pt_ref, *refs, scale, lam_init, n_heads_f, n_heads_d, ppb):
    fk_refs, fv_refs = refs[0:ppb], refs[ppb:2 * ppb]
    dk_refs, dv_refs = refs[2 * ppb:3 * ppb], refs[3 * ppb:4 * ppb]
    (bias_ref, qf_ref, knf_ref, vnf_ref, qd_ref, knd_ref, vnd_ref, zf_ref, zd_ref, g_ref,
     lq1, lk1, lq2, lk2, xa_ref, xb_ref, mf, lf, af, md, ld, ad) = refs[4 * ppb:]
    j = pl.program_id(1)
    hp = 2 * n_heads_d
    bias_row0 = (j * ppb) % 8

    def bf(x):
        return x.astype(BF16)

    @pl.when(j == 0)
    def _():
        qn = bf(qf_ref[...]).astype(F32)
        s_new = jnp.sum(qn * bf(knf_ref[...]).astype(F32), axis=1, keepdims=True) * scale
        mf[...] = s_new
        lf[...] = jnp.ones_like(lf)
        af[...] = bf(vnf_ref[...]).astype(F32)
        for c in range(2):
            qn = bf(qd_ref[c]).astype(F32)
            s_new = jnp.sum(qn * bf(knd_ref[c]).astype(F32), axis=1, keepdims=True) * scale
            md[c * hp:(c + 1) * hp, :] = s_new
            ad[c * hp:(c + 1) * hp, :] = bf(vnd_ref[...]).astype(F32)
        ld[...] = jnp.ones_like(ld)

    qf = bf(qf_ref[...])
    s = jnp.concatenate(
        [lax.dot_general(qf, bf(r[...]), _NT, preferred_element_type=F32) * scale
         + bias_ref[pl.ds(bias_row0 + p, 1), :] for p, r in enumerate(fk_refs)], axis=1)
    pw = fk_refs[0].shape[0]
    row = lax.broadcasted_iota(jnp.int32, s.shape, 0)
    col = lax.broadcasted_iota(jnp.int32, s.shape, 1)
    s = jnp.where((col % n_heads_f) == row, s, -jnp.inf)
    m_old = mf[...]
    m_new = jnp.maximum(m_old, jnp.max(s, axis=1, keepdims=True))
    alpha = jnp.exp(m_old - m_new)
    p = jnp.exp(s - m_new)
    mf[...] = m_new
    lf[...] = alpha * lf[...] + jnp.sum(p, axis=1, keepdims=True)
    pb = bf(p)
    af[...] = alpha * af[...] + sum(
        jnp.dot(pb[:, i * pw:(i + 1) * pw], bf(r[...]), preferred_element_type=F32)
        for i, r in enumerate(fv_refs))

    pw = dk_refs[0].shape[0] // 2
    qd = [bf(qd_ref[c]) for c in range(2)]
    s = jnp.concatenate(
        [jnp.concatenate(
            [lax.dot_general(qd[c], bf(r[pl.ds(c, pw, stride=2), :]), _NT,
                             preferred_element_type=F32) for c in range(2)], axis=0)
         for r in dk_refs], axis=1) * scale
    row = lax.broadcasted_iota(jnp.int32, s.shape, 0)
    col = lax.broadcasted_iota(jnp.int32, s.shape, 1)
    s = jnp.where((col % n_heads_d) == (row % hp), s, -jnp.inf)
    m_old = md[...]
    m_new = jnp.maximum(m_old, jnp.max(s, axis=1, keepdims=True))
    alpha = jnp.exp(m_old - m_new)
    p = jnp.exp(s - m_new)
    md[...] = m_new
    ld[...] = alpha * ld[...] + jnp.sum(p, axis=1, keepdims=True)
    pb = bf(p)
    ad[...] = alpha * ad[...] + sum(
        jnp.dot(pb[:, i * pw:(i + 1) * pw], bf(r[...]), preferred_element_type=F32)
        for i, r in enumerate(dv_refs))

    @pl.when(j == pl.num_programs(1) - 1)
    def _():
        z = zf_ref[...]
        xa_ref[...] = ((af[...] / lf[...]) * (z * jax.nn.sigmoid(z))).astype(xa_ref.dtype)
        lam = _diff_lambda(lq1[...], lk1[...], lq2[...], lk2[...], lam_init)
        o = ad[0:hp, :] / ld[0:hp, :] - lam * (ad[hp:2 * hp, :] / ld[hp:2 * hp, :])
        o = o * lax.rsqrt(jnp.mean(o * o, axis=1, keepdims=True) + EPS) * g_ref[...]
        o = o * (1.0 - lam_init)
        z = zd_ref[...]
        xb_ref[...] = (o * (z * jax.nn.sigmoid(z))).astype(xb_ref.dtype)


def _decode_attention(page_table, fk, fv, dk, dv, bias, qf, knf, vnf, qd, knd, vnd, zf, zd,
                      g, lams, *, layer, n_pool, lam_init, ppb):
    db, n_pages = page_table.shape
    n_heads_f = qf.shape[1]
    hp = qd.shape[2]
    n_heads_d = hp // 2
    dvw = 2 * HEAD_DIM
    base = layer * n_pool
    assert 8 % ppb == 0 and n_pages % ppb == 0

    def pages(a):
        return [pl.BlockSpec((None,) + a.shape[1:],
                             lambda b, j, pt, p=p: (base + pt[b, j * ppb + p], 0, 0))
                for p in range(ppb)]

    def per_b(shape):
        nd = len(shape)
        return pl.BlockSpec((None,) + shape, lambda b, j, pt: (b,) + (0,) * nd)

    vec = pl.BlockSpec((1, HEAD_DIM), lambda b, j, pt: (0, 0))
    return pl.pallas_call(
        functools.partial(_decode_kernel, scale=HEAD_DIM ** -0.5, lam_init=lam_init,
                          n_heads_f=n_heads_f, n_heads_d=n_heads_d, ppb=ppb),
        grid_spec=pltpu.PrefetchScalarGridSpec(
            num_scalar_prefetch=1,
            grid=(db, n_pages // ppb),
            in_specs=pages(fk) + pages(fv) + pages(dk) + pages(dv) + [
                      pl.BlockSpec((None, 8, bias.shape[2]), lambda b, j, pt: (b, (j * ppb) // 8, 0)),
                      per_b(qf.shape[1:]), per_b(knf.shape[1:]), per_b(vnf.shape[1:]),
                      per_b(qd.shape[1:]), per_b(knd.shape[1:]), per_b(vnd.shape[1:]),
                      per_b(zf.shape[1:]), per_b(zd.shape[1:]),
                      pl.BlockSpec((1, dvw), lambda b, j, pt: (0, 0)),
                      vec, vec, vec, vec],
            out_specs=[per_b((n_heads_f, HEAD_DIM)), per_b((hp, dvw))],
            scratch_shapes=[pltpu.VMEM((n_heads_f, 1), F32), pltpu.VMEM((n_heads_f, 1), F32),
                            pltpu.VMEM((n_heads_f, HEAD_DIM), F32),
                            pltpu.VMEM((2 * hp, 1), F32), pltpu.VMEM((2 * hp, 1), F32),
                            pltpu.VMEM((2 * hp, dvw), F32)]),
        out_shape=[jax.ShapeDtypeStruct((db, n_heads_f, HEAD_DIM), BF16),
                   jax.ShapeDtypeStruct((db, hp, dvw), BF16)],
        compiler_params=pltpu.CompilerParams(
            dimension_semantics=("arbitrary", "arbitrary"), vmem_limit_bytes=V7X_VMEM_LIMIT),
        name="decode_attention",
    )(page_table, *([fk] * ppb + [fv] * ppb + [dk] * ppb + [dv] * ppb),
      bias, qf, knf, vnf, qd, knd, vnd, zf, zd, g, *lams)


def _rope_tables(pos):
    half = HEAD_DIM // 2
    inv = 1.0 / (ROPE_THETA ** (jnp.arange(half, dtype=F32) * 2.0 / HEAD_DIM))
    ang = pos.astype(F32)[:, None] * inv[None, :]
    cos, sin = jnp.cos(ang), jnp.sin(ang)
    return jnp.concatenate([cos, cos], axis=1), jnp.concatenate([-sin, sin], axis=1)


def _in_project(x, w_in_t, layer, b_f, cos2, sin2, *, bm, fw, dw, d, stack_depth=0, stacked=None):
    n_heads_f = b_f.shape[0]
    mm = functools.partial(_matmul, x, w_in_t, layer, bm=bm, bn=1024, w_rows=True)
    rope = dict(epilogue="rope", extra=(cos2, sin2))
    tail0 = 4 * fw + n_heads_f

    def kv(name, **kw):
        prev = None if stacked is None else stacked[name]
        return mm([F32, BF16], n=kw.pop("n"), name="proj_" + name, stack_depth=stack_depth,
                  stacked=prev, **kw)

    qa, = mm([BF16], w0=0, n=fw, name="proj_qa")
    ka, ka16 = kv("ka", w0=fw, n=fw)
    va, va16 = kv("va", w0=2 * fw, n=fw)
    za, = mm([F32], w0=3 * fw, n=fw, name="proj_za")
    b_pad = jnp.pad(b_f.astype(F32), (0, LANES - n_heads_f)).reshape(1, LANES)
    logf_pad, = _matmul(x, w_in_t, layer, [F32], bm=bm, bn=LANES, n=LANES, w0=4 * fw, w_rows=True,
                        epilogue="logsig", extra=(b_pad,), name="proj_fa")
    qb, = mm([BF16], w0=tail0, n=dw, name="proj_qb", **rope)
    kb, kb16 = kv("kb", w0=tail0 + dw, n=dw, **rope)
    vb, vb16 = kv("vb", w0=tail0 + 2 * dw, n=dw)
    zgb, = mm([F32], w0=tail0 + 3 * dw, n=dw + 2 * d, name="proj_gates")
    return dict(qa=qa, ka=ka, ka16=ka16, va=va, va16=va16, za=za, logf_pad=logf_pad, qb=qb, kb=kb,
                kb16=kb16, vb=vb, vb16=vb16, zgb=zgb)


def kernel(x_prompt, x_sample, cache_fox_k, cache_fox_v, cache_fox_logf, cache_diff_k, cache_diff_v,
           page_table, meta_tokens, w_in, b_forget, lambda_q1, lambda_k1, lambda_q2, lambda_k2,
           diff_norm_g, w_out_fox, w_out_diff, w_out, ln_g, ln_b):
    nb, seq, d = x_prompt.shape
    depth = w_in.shape[0]
    t = seq + N_META
    n_heads_f = b_forget.shape[1]
    fw = n_heads_f * HEAD_DIM
    dw = w_out_diff.shape[1]
    n_heads_d = dw // (2 * HEAD_DIM)
    db, n_new, _ = x_sample.shape
    n_pool, page = cache_fox_k.shape[1], cache_fox_k.shape[2]
    n_pages = page_table.shape[1]
    past = n_pages * page
    assert n_new == 1 and db <= ROW_PAD and n_pages % 8 == 0
    alpha = (2 * depth) ** 0.25
    blk = 256
    bm = t // 3
    bm_ln = max(r for r in range(ROW_PAD, 193, ROW_PAD) if (nb * t) % r == 0)
    assert t % 3 == 0 and bm % ROW_PAD == 0 and (t - N_META) % blk == 0

    cos_p, sin_p = _rope_tables(jnp.arange(t, dtype=jnp.int32))
    cos_s, sin_s = _rope_tables(jnp.full((ROW_PAD,), past, dtype=jnp.int32))

    meta = jnp.broadcast_to(meta_tokens.astype(x_prompt.dtype)[None], (nb, N_META, d))
    h = jnp.concatenate([meta, x_prompt], axis=1).reshape(nb * t, d)
    h16 = h.astype(BF16)
    hs = jnp.pad(x_sample.reshape(db, d), ((0, ROW_PAD - db), (0, 0)))
    hs16 = hs.astype(BF16)

    fk_pool = cache_fox_k.reshape(depth * n_pool, page * n_heads_f, HEAD_DIM)
    fv_pool = cache_fox_v.reshape(depth * n_pool, page * n_heads_f, HEAD_DIM)
    dk_pool = cache_diff_k.reshape(depth * n_pool, page * n_heads_d * 2, HEAD_DIM)
    dv_pool = cache_diff_v.reshape(depth * n_pool, page * n_heads_d, 2 * HEAD_DIM)
    lf_pool = cache_fox_logf.astype(F32).reshape(depth, n_pool, page * n_heads_f)

    w_in_t = jnp.swapaxes(w_in, 1, 2).astype(BF16)
    w_oa16, w_ob16, w_o16 = w_out_fox.astype(BF16), w_out_diff.astype(BF16), w_out.astype(BF16)

    leaves_p, lf_p = None, []
    outs_s = {k: [] for k in ("ka", "va", "lf", "kb", "vb")}
    row = lambda a, l: a[l].astype(F32).reshape(1, -1)

    for l in range(depth):
        lam_init = _lambda_init(l)
        lams = (row(lambda_q1, l), row(lambda_k1, l), row(lambda_q2, l), row(lambda_k2, l))
        g = row(diff_norm_g, l)
        lng, lnb = row(ln_g, l), row(ln_b, l)
        proj = functools.partial(_in_project, w_in_t=w_in_t, layer=l, b_f=b_forget[l], fw=fw, dw=dw, d=d)
        merge = functools.partial(_gated_merge, w_oa=w_oa16, w_ob=w_ob16, layer=l,
                                  ga_col0=dw, gb_col0=dw + d, bn=1024)

        p = proj(h16, cos2=cos_p, sin2=sin_p, bm=bm, stack_depth=depth, stacked=leaves_p)
        leaves_p = {k: p[k] for k in ("ka", "va", "kb", "vb")}
        r3 = lambda a: a.reshape(nb, t, -1)
        c, cts, ctm = _prompt_cumsum(r3(p["logf_pad"]), n_heads_f, blk)
        xa = _fox_attention(r3(p["qa"]), r3(p["ka16"]), r3(p["va16"]), r3(p["za"]), c, cts, ctm,
                            n_heads=n_heads_f, blk=blk)
        xb = _diff_attention(r3(p["qb"]), r3(p["kb16"]), r3(p["vb16"]), r3(p["zgb"]), 0, g, lams,
                             n_heads=n_heads_d, blk=blk, lam_init=lam_init)
        m = merge(xa.reshape(nb * t, fw), xb.reshape(nb * t, dw), zg=p["zgb"], bm=bm)
        h, h16 = _out_ln(m, w_o16, l, h, lng, lnb, bm=bm, bm_ln=bm_ln, alpha=alpha)
        lf_p.append(p["logf_pad"][:, :n_heads_f].reshape(nb, t, n_heads_f))

        s = proj(hs16, cos2=cos_s, sin2=sin_s, bm=ROW_PAD)
        lf_new = jnp.tile(s["logf_pad"][:, :n_heads_f], (1, LANES // n_heads_f))
        bias = _decode_bias(page_table, lf_pool, l, lf_new, n_heads=n_heads_f)
        hp = 2 * n_heads_d

        def maps(a):
            a = a.reshape(ROW_PAD, n_heads_d, 2, HEAD_DIM).transpose(0, 2, 1, 3)
            return jnp.pad(a, ((0, 0), (0, 0), (0, hp - n_heads_d), (0, 0)))

        def vrows(a):
            a = a.reshape(ROW_PAD, n_heads_d, 2 * HEAD_DIM)
            return jnp.pad(a, ((0, 0), (0, hp - n_heads_d), (0, 0)))

        xa_s, xb_s = _decode_attention(
            page_table, fk_pool, fv_pool, dk_pool, dv_pool, bias,
            s["qa"].astype(F32).reshape(ROW_PAD, n_heads_f, HEAD_DIM),
            s["ka"].reshape(ROW_PAD, n_heads_f, HEAD_DIM),
            s["va"].reshape(ROW_PAD, n_heads_f, HEAD_DIM),
            maps(s["qb"].astype(F32)), maps(s["kb"]), vrows(s["vb"]),
            s["za"].reshape(ROW_PAD, n_heads_f, HEAD_DIM), vrows(s["zgb"][:, :dw]),
            g, lams, layer=l, n_pool=n_pool, lam_init=lam_init, ppb=4)
        xa_s = jnp.pad(xa_s.reshape(db, fw), ((0, ROW_PAD - db), (0, 0)))
        xb_s = jnp.pad(xb_s[:, :n_heads_d].reshape(db, dw), ((0, ROW_PAD - db), (0, 0)))
        ms = merge(xa_s, xb_s, zg=s["zgb"], bm=ROW_PAD)
        hs, hs16 = _out_ln(ms, w_o16, l, hs, lng, lnb, bm=ROW_PAD, bm_ln=ROW_PAD, alpha=alpha)
        outs_s["ka"].append(s["ka"][:db].reshape(db, 1, n_heads_f, HEAD_DIM))
        outs_s["va"].append(s["va"][:db].reshape(db, 1, n_heads_f, HEAD_DIM))
        outs_s["lf"].append(s["logf_pad"][:db, :n_heads_f].reshape(db, 1, n_heads_f))
        outs_s["kb"].append(s["kb"][:db].reshape(db, 1, n_heads_d, 2, HEAD_DIM))
        outs_s["vb"].append(s["vb"][:db].reshape(db, 1, n_heads_d, 2 * HEAD_DIM))

    y_prompt = h.reshape(nb, t, d)[:, N_META:]
    y_sample = hs[:db].reshape(db, 1, d)
    st = jnp.stack
    return (y_prompt, y_sample,
            leaves_p["ka"].reshape(depth, nb, t, n_heads_f, HEAD_DIM),
            leaves_p["va"].reshape(depth, nb, t, n_heads_f, HEAD_DIM),
            st(lf_p),
            leaves_p["kb"].reshape(depth, nb, t, n_heads_d, 2, HEAD_DIM),
            leaves_p["vb"].reshape(depth, nb, t, n_heads_d, 2 * HEAD_DIM),
            st(outs_s["ka"]), st(outs_s["va"]), st(outs_s["lf"]), st(outs_s["kb"]), st(outs_s["vb"]))
```

```python
import functools
import math

import jax
import jax.numpy as jnp
from jax import lax
from jax.experimental import pallas as pl
from jax.experimental.pallas import tpu as pltpu

F32 = jnp.float32
BF16 = jnp.bfloat16

HEAD_DIM = 128
N_META = 16
ROPE_THETA = 10000.0
EPS = 1e-5
LANES = 128
ROW_PAD = 16
V7X_VMEM_LIMIT = 56 * 1024 * 1024

_NT = (((1,), (1,)), ((), ()))


def _lambda_init(layer):
    return 0.8 - 0.6 * math.exp(-0.3 * layer)


def _split3(x):
    hi = x.astype(BF16)
    r1 = x - hi.astype(F32)
    mid = r1.astype(BF16)
    lo = (r1 - mid.astype(F32)).astype(BF16)
    return hi, mid, lo


def _dot3(a_bf16, x_f32):
    return sum(jnp.dot(a_bf16, p, preferred_element_type=F32) for p in _split3(x_f32))


def _dot3_left(x_f32, a_bf16):
    return sum(jnp.dot(p, a_bf16, preferred_element_type=F32) for p in _split3(x_f32))


def _diff_lambda(lq1, lk1, lq2, lk2, lam_init):
    e1 = jnp.exp(jnp.sum(lq1 * lk1, axis=1, keepdims=True))
    e2 = jnp.exp(jnp.sum(lq2 * lk2, axis=1, keepdims=True))
    return e1 - e2 + lam_init


def _mm_kernel(x_ref, w_ref, *rest, epilogue, n_out, alpha, w_rows, shift):
    if shift:
        w_next_ref, w_tile_ref = rest[0], rest[-1]
        rest = rest[1:-1]
        bn = w_tile_ref.shape[0]

        @pl.when(pl.program_id(1) == 0)
        def _():
            w_tile_ref[0:bn - shift, :] = w_ref[shift:bn, :]
            w_tile_ref[bn - shift:bn, :] = w_next_ref[...]

        w = w_tile_ref[...]
    else:
        w = w_ref[...]
    outs = rest[len(rest) - n_out:]
    extra = rest[:len(rest) - n_out]
    if w_rows:
        acc = lax.dot_general(x_ref[...], w, _NT, preferred_element_type=F32)
    else:
        acc = jnp.dot(x_ref[...], w, preferred_element_type=F32)
    if epilogue == "residual":
        acc = alpha * extra[0][...] + acc
    if epilogue == "logsig":
        z = acc + extra[0][...]
        acc = jnp.minimum(z, 0.0) - jnp.log1p(jnp.exp(-jnp.abs(z)))
    if epilogue == "rope":
        cos = extra[0][...]
        sin = extra[1][...]
        for g in range(acc.shape[1] // HEAD_DIM):
            sl = slice(g * HEAD_DIM, (g + 1) * HEAD_DIM)
            xg = acc[:, sl]
            r = xg * cos + pltpu.roll(xg, HEAD_DIM // 2, 1) * sin
            for o in outs:
                o[:, sl] = r.astype(o.dtype)
    else:
        for o in outs:
            o[...] = acc.astype(o.dtype)


def _matmul(x, w, layer, out_dtypes, *, bm, bn, n, w0=0, w_rows=False, epilogue="none", extra=(),
            alpha=None, name="mm", stack_depth=0, stacked=None):
    m, k = x.shape
    shift = w0 % bn
    assert m % bm == 0 and n % bn == 0 and shift in (0, ROW_PAD) and (w_rows or not shift)
    b0 = w0 // bn
    if w_rows:
        w_specs = [pl.BlockSpec((None, bn, k), lambda j, i: (layer, b0 + j, 0))]
        if shift:
            per = bn // shift
            w_specs.append(pl.BlockSpec((None, shift, k), lambda j, i: (layer, (b0 + j + 1) * per, 0)))
    else:
        w_specs = [pl.BlockSpec((None, k, bn), lambda j, i: (layer, 0, b0 + j))]
    in_specs = [pl.BlockSpec((bm, k), lambda j, i: (i, 0))] + w_specs
    if epilogue == "logsig":
        in_specs.append(pl.BlockSpec((1, bn), lambda j, i: (0, j)))
    elif epilogue == "rope":
        nt = extra[0].shape[0] // bm
        in_specs += [pl.BlockSpec((bm, HEAD_DIM), lambda j, i: (i % nt, 0))] * 2
    elif epilogue == "residual":
        in_specs.append(pl.BlockSpec((bm, bn), lambda j, i: (i, j)))
    out_specs = [pl.BlockSpec((bm, bn), lambda j, i: (i, j)) for _ in out_dtypes]
    out_shape = [jax.ShapeDtypeStruct((m, n), d) for d in out_dtypes]
    aliases, stacked_in = {}, ()
    if stack_depth:
        r0 = layer * (m // bm)
        out_specs[0] = pl.BlockSpec((bm, bn), lambda j, i: (r0 + i, j))
        out_shape[0] = jax.ShapeDtypeStruct((stack_depth * m, n), out_dtypes[0])
        if stacked is not None:
            aliases = {len(in_specs): 0}
            in_specs.append(pl.BlockSpec(memory_space=pl.ANY))
            stacked_in = (stacked,)
    outs = pl.pallas_call(
        functools.partial(_mm_kernel, epilogue=epilogue, n_out=len(out_dtypes), alpha=alpha,
                          w_rows=w_rows, shift=shift),
        grid=(n // bn, m // bm),
        in_specs=in_specs,
        out_specs=out_specs,
        out_shape=out_shape,
        input_output_aliases=aliases,
        scratch_shapes=[pltpu.VMEM((bn, k), BF16)] if shift else [],
        compiler_params=pltpu.CompilerParams(
            dimension_semantics=("arbitrary", "arbitrary"), vmem_limit_bytes=V7X_VMEM_LIMIT),
        name=name,
    )(x, *([w] * len(w_specs)), *extra, *stacked_in)
    return outs


def _cum_kernel(lf_ref, c_ref, cts_ref, ctm_ref, *, n_chunks, chunk):
    def tri(n):
        r = lax.broadcasted_iota(jnp.int32, (n, n), 0)
        c = lax.broadcasted_iota(jnp.int32, (n, n), 1)
        return (c <= r).astype(BF16)

    c0 = _dot3(tri(LANES), lf_ref[0:LANES, :])
    c_ref[0:LANES, :] = c0
    ctm_ref[...] = c0.T[0:ctm_ref.shape[0], :]
    carry = c0[N_META - 1:N_META, :]
    tri_c = tri(chunk)
    for j in range(n_chunks):
        r0 = N_META + j * chunk
        cj = _dot3(tri_c, lf_ref[r0:r0 + chunk, :]) + carry
        c_ref[r0:r0 + chunk, :] = cj
        cts_ref[:, j * chunk:(j + 1) * chunk] = cj.T[0:cts_ref.shape[0], :]
        carry = cj[chunk - 1:chunk, :]


def _prompt_cumsum(logf_pad, n_heads, chunk):
    b, t, _ = logf_pad.shape
    n_chunks = (t - N_META) // chunk
    return pl.pallas_call(
        functools.partial(_cum_kernel, n_chunks=n_chunks, chunk=chunk),
        grid=(b,),
        in_specs=[pl.BlockSpec((None, t, LANES), lambda i: (i, 0, 0))],
        out_specs=[pl.BlockSpec((None, t, LANES), lambda i: (i, 0, 0)),
                   pl.BlockSpec((None, n_heads, t - N_META), lambda i: (i, 0, 0)),
                   pl.BlockSpec((None, n_heads, LANES), lambda i: (i, 0, 0))],
        out_shape=[jax.ShapeDtypeStruct((b, t, LANES), F32),
                   jax.ShapeDtypeStruct((b, n_heads, t - N_META), F32),
                   jax.ShapeDtypeStruct((b, n_heads, LANES), F32)],
        compiler_params=pltpu.CompilerParams(dimension_semantics=("arbitrary",)),
        name="prompt_cumsum",
    )(logf_pad)


LOG2E = math.log2(math.e)


def _causal_mask(n):
    r = lax.broadcasted_iota(jnp.int32, (n, n), 0)
    c = lax.broadcasted_iota(jnp.int32, (n, n), 1)
    return c <= r


def _mask_tail(s, mask):
    w = mask.shape[1]
    tail = jnp.where(mask, s[:, s.shape[1] - w:], -jnp.inf)
    if s.shape[1] == w:
        return tail
    return jnp.concatenate([s[:, :s.shape[1] - w], tail], axis=1)


def _prefix_softmax(s_head, s_tok, v_head, v_tok):
    m = jnp.max(s_head, axis=1, keepdims=True)
    if s_tok is not None:
        m = jnp.maximum(m, jnp.max(s_tok, axis=1, keepdims=True))
    p = jnp.exp2(s_head - m)
    l = jnp.sum(p, axis=1, keepdims=True)
    acc = jnp.dot(p.astype(BF16), v_head, preferred_element_type=F32)
    if s_tok is not None:
        p = jnp.exp2(s_tok - m)
        l = l + jnp.sum(p, axis=1, keepdims=True)
        acc = acc + jnp.dot(p.astype(BF16), v_tok, preferred_element_type=F32)
    return l, acc


def _fox_kernel(q_ref, k_ref, v_ref, z_ref, c_ref, cts_ref, ctm_ref, o_ref, *, scale, n_blk, blk, hps):
    diag = _causal_mask(blk)
    meta_cols = lax.broadcasted_iota(jnp.int32, (blk, LANES), 1) < N_META

    def attend(sub, r0, n_rows, n_store, n_tok, head_mask):
        h = pl.program_id(1) * hps + sub
        sl = slice(sub * HEAD_DIM, (sub + 1) * HEAD_DIM)
        q = q_ref[r0:r0 + n_rows, sl]
        cb = c_ref[r0:r0 + n_rows, :]
        lane = lax.broadcasted_iota(jnp.int32, cb.shape, 1)
        cq = jnp.sum(jnp.where(lane == h, cb, 0.0), axis=1, keepdims=True) * LOG2E

        def scores(k, ck):
            s = lax.dot_general(q, k, _NT, preferred_element_type=F32)
            return s * (scale * LOG2E) + (cq - ck * LOG2E)

        s_head = jnp.where(head_mask, scores(k_ref[0:LANES, sl], ctm_ref[sub]), -jnp.inf)
        s_tok = v_tok = None
        if n_tok:
            s_tok = _mask_tail(scores(k_ref[N_META:N_META + n_tok, sl], cts_ref[sub, :, 0:n_tok]), diag)
            v_tok = v_ref[N_META:N_META + n_tok, sl]
        l, acc = _prefix_softmax(s_head, s_tok, v_ref[0:LANES, sl], v_tok)
        z = z_ref[r0:r0 + n_rows, sl]
        x = (acc / l) * (z * jax.nn.sigmoid(z))
        o_ref[r0:r0 + n_store, sl] = x[0:n_store].astype(o_ref.dtype)

    for sub in range(hps):
        attend(sub, 0, LANES, N_META, 0, _causal_mask(LANES))
        for i in range(n_blk):
            attend(sub, N_META + i * blk, blk, blk, (i + 1) * blk, meta_cols)


def _fox_attention(q, k, v, zg, c, cts, ctm, *, n_heads, blk, hps=2):
    b, t, _ = q.shape
    n_blk = (t - N_META) // blk
    assert n_heads % hps == 0
    hd = pl.BlockSpec((None, t, hps * HEAD_DIM), lambda i, h: (i, 0, h))
    return pl.pallas_call(
        functools.partial(_fox_kernel, scale=HEAD_DIM ** -0.5, n_blk=n_blk, blk=blk, hps=hps),
        grid=(b, n_heads // hps),
        in_specs=[hd, hd, hd, hd,
                  pl.BlockSpec((None, t, LANES), lambda i, h: (i, 0, 0)),
                  pl.BlockSpec((None, hps, 1, t - N_META), lambda i, h: (i, h, 0, 0)),
                  pl.BlockSpec((None, hps, 1, LANES), lambda i, h: (i, h, 0, 0))],
        out_specs=hd,
        out_shape=jax.ShapeDtypeStruct((b, t, n_heads * HEAD_DIM), BF16),
        compiler_params=pltpu.CompilerParams(dimension_semantics=("arbitrary", "arbitrary")),
        name="fox_attention",
    )(q, k, v, zg, c, cts.reshape(b, n_heads, 1, t - N_META), ctm.reshape(b, n_heads, 1, LANES))


def _diff_kernel(q_ref, k_ref, v_ref, z_ref, g_ref, lq1, lk1, lq2, lk2, o_ref, *,
                 scale, n_blk, blk, lam_init):
    lam = _diff_lambda(lq1[...], lk1[...], lq2[...], lk2[...], lam_init)
    diag = _causal_mask(blk)
    meta_cols = lax.broadcasted_iota(jnp.int32, (blk, LANES), 1) < N_META

    def attend(r0, n_rows, n_store, n_tok, head_mask):
        v_head = v_ref[0:LANES, :]
        v_tok = v_ref[N_META:N_META + n_tok, :] if n_tok else None
        o_maps = []
        for c in range(2):
            sl = slice(c * HEAD_DIM, (c + 1) * HEAD_DIM)
            q = q_ref[r0:r0 + n_rows, sl]

            def scores(k):
                return lax.dot_general(q, k, _NT, preferred_element_type=F32) * (scale * LOG2E)

            s_head = jnp.where(head_mask, scores(k_ref[0:LANES, sl]), -jnp.inf)
            s_tok = _mask_tail(scores(k_ref[N_META:N_META + n_tok, sl]), diag) if n_tok else None
            l, acc = _prefix_softmax(s_head, s_tok, v_head, v_tok)
            o_maps.append(acc / l)
        o = o_maps[0] - lam * o_maps[1]
        o = o * lax.rsqrt(jnp.mean(o * o, axis=1, keepdims=True) + EPS) * g_ref[...]
        o = o * (1.0 - lam_init)
        z = z_ref[r0:r0 + n_rows, :]
        x = o * (z * jax.nn.sigmoid(z))
        o_ref[r0:r0 + n_store, :] = x[0:n_store].astype(o_ref.dtype)

    attend(0, LANES, N_META, 0, _causal_mask(LANES))
    for i in range(n_blk):
        attend(N_META + i * blk, blk, blk, (i + 1) * blk, meta_cols)


def _diff_attention(q, k, v, zg, z_col0, g, lams, *, n_heads, blk, lam_init):
    b, t, _ = q.shape
    dv = 2 * HEAD_DIM
    n_blk = (t - N_META) // blk
    zoff = z_col0 // dv
    hd = pl.BlockSpec((None, t, dv), lambda i, h: (i, 0, h))
    vec = pl.BlockSpec((1, HEAD_DIM), lambda i, h: (0, 0))
    return pl.pallas_call(
        functools.partial(_diff_kernel, scale=HEAD_DIM ** -0.5, n_blk=n_blk, blk=blk,
                          lam_init=lam_init),
        grid=(b, n_heads),
        in_specs=[hd, hd, hd,
                  pl.BlockSpec((None, t, dv), lambda i, h: (i, 0, h + zoff)),
                  pl.BlockSpec((1, dv), lambda i, h: (0, 0)),
                  vec, vec, vec, vec],
        out_specs=hd,
        out_shape=jax.ShapeDtypeStruct((b, t, n_heads * dv), BF16),
        compiler_params=pltpu.CompilerParams(dimension_semantics=("arbitrary", "arbitrary")),
        name="diff_attention",
    )(q, k, v, zg, g, *lams)


def _gate_kernel(xa_ref, xb_ref, wa_ref, wb_ref, ga_ref, gb_ref, o_ref):
    br_a = jnp.dot(xa_ref[...], wa_ref[...], preferred_element_type=F32)
    br_b = jnp.dot(xb_ref[...], wb_ref[...], preferred_element_type=F32)
    m = jax.nn.sigmoid(ga_ref[...]) * br_a + jax.nn.sigmoid(gb_ref[...]) * br_b
    o_ref[...] = m.astype(o_ref.dtype)


def _gated_merge(xa, xb, w_oa, w_ob, layer, zg, ga_col0, gb_col0, *, bm, bn):
    m, k = xa.shape
    n = w_oa.shape[2]
    xs = pl.BlockSpec((bm, k), lambda j, i: (i, 0))
    ws = pl.BlockSpec((None, k, bn), lambda j, i: (layer, 0, j))
    ga_off, gb_off = ga_col0 // bn, gb_col0 // bn
    assert ga_col0 % bn == 0 and gb_col0 % bn == 0
    return pl.pallas_call(
        _gate_kernel,
        grid=(n // bn, m // bm),
        in_specs=[xs, xs, ws, ws,
                  pl.BlockSpec((bm, bn), lambda j, i: (i, j + ga_off)),
                  pl.BlockSpec((bm, bn), lambda j, i: (i, j + gb_off))],
        out_specs=pl.BlockSpec((bm, bn), lambda j, i: (i, j)),
        out_shape=jax.ShapeDtypeStruct((m, n), BF16),
        compiler_params=pltpu.CompilerParams(
            dimension_semantics=("arbitrary", "arbitrary"), vmem_limit_bytes=V7X_VMEM_LIMIT),
        name="gated_merge",
    )(xa, xb, w_oa, w_ob, zg, zg)


def _ln_kernel(u_ref, g_ref, b_ref, y_ref, yb_ref):
    u = u_ref[...]
    mu = jnp.mean(u, axis=1, keepdims=True)
    d = u - mu
    var = jnp.mean(d * d, axis=1, keepdims=True)
    y = d * lax.rsqrt(var + EPS) * g_ref[...] + b_ref[...]
    y_ref[...] = y
    yb_ref[...] = y.astype(yb_ref.dtype)


def _layer_norm(u, ln_g, ln_b, *, bm):
    m, n = u.shape
    row = pl.BlockSpec((bm, n), lambda i: (i, 0))
    vec = pl.BlockSpec((1, n), lambda i: (0, 0))
    return pl.pallas_call(
        _ln_kernel,
        grid=(m // bm,),
        in_specs=[row, vec, vec],
        out_specs=[row, row],
        out_shape=[jax.ShapeDtypeStruct((m, n), F32), jax.ShapeDtypeStruct((m, n), BF16)],
        compiler_params=pltpu.CompilerParams(dimension_semantics=("arbitrary",)),
        name="layer_norm",
    )(u, ln_g, ln_b)


def _out_ln(m_bf16, w_o, layer, h, ln_g, ln_b, *, bm, bm_ln, alpha):
    u, = _matmul(m_bf16, w_o, layer, [F32], bm=bm, bn=1024, n=w_o.shape[2], epilogue="residual",
                 extra=(h,), alpha=alpha, name="out_proj")
    return _layer_norm(u, ln_g, ln_b, bm=bm_ln)


def _decode_bias_kernel(pt_ref, pool_ref, new_ref, o_ref, g_ref, *, n_pages, n_heads):
    b = pl.program_id(0)
    w = pool_ref.shape[1]

    def gather(j, carry):
        g_ref[pl.ds(j, 1), :] = pool_ref[pl.ds(pt_ref[b, j], 1), :]
        return carry

    lax.fori_loop(0, n_pages, gather, 0)

    r = lax.broadcasted_iota(jnp.int32, (LANES, LANES), 0)
    c = lax.broadcasted_iota(jnp.int32, (LANES, LANES), 1)
    same_head = (r % n_heads) == (c % n_heads)
    head_sum = same_head.astype(BF16)
    later_in_block = (same_head & (r > c)).astype(BF16)
    after = jnp.zeros((n_pages, LANES), F32)
    for kb in reversed(range(w // LANES)):
        sl = slice(kb * LANES, (kb + 1) * LANES)
        gk = g_ref[:, sl]
        o_ref[:, sl] = _dot3_left(gk, later_in_block) + _dot3_left(after, head_sum)
        after = after + gk
    total = _dot3_left(after, head_sum)
    pr = lax.broadcasted_iota(jnp.int32, (n_pages, n_pages), 0)
    pc = lax.broadcasted_iota(jnp.int32, (n_pages, n_pages), 1)
    rest = _dot3((pc > pr).astype(BF16), total) + new_ref[pl.ds(b, 1), :]
    for kb in range(w // LANES):
        sl = slice(kb * LANES, (kb + 1) * LANES)
        o_ref[:, sl] = o_ref[:, sl] + rest


def _decode_bias(page_table, pool_flat, layer, logf_new_pad, *, n_heads):
    db, n_pages = page_table.shape
    _, n_pool, w = pool_flat.shape
    return pl.pallas_call(
        functools.partial(_decode_bias_kernel, n_pages=n_pages, n_heads=n_heads),
        grid_spec=pltpu.PrefetchScalarGridSpec(
            num_scalar_prefetch=1,
            grid=(db,),
            in_specs=[pl.BlockSpec((None, n_pool, w), lambda b, pt: (layer, 0, 0)),
                      pl.BlockSpec(logf_new_pad.shape, lambda b, pt: (0, 0))],
            out_specs=pl.BlockSpec((None, n_pages, w), lambda b, pt: (b, 0, 0)),
            scratch_shapes=[pltpu.VMEM((n_pages, w), F32)]),
        out_shape=jax.ShapeDtypeStruct((db, n_pages, w), F32),
        compiler_params=pltpu.CompilerParams(
            dimension_semantics=("arbitrary",), vmem_limit_bytes=V7X_VMEM_LIMIT),
        name="decode_bias",
    )(page_table, pool_flat, logf_new_pad)


def _decode_kernel(pt_ref, *refs, scale, lam_init, n_heads_f, n_heads_d, ppb):
    fk_refs, fv_refs = refs[0:ppb], refs[ppb:2 * ppb]
    dk_refs, dv_refs = refs[2 * ppb:3 * ppb], refs[3 * ppb:4 * ppb]
    (bias_ref, qf_ref, knf_ref, vnf_ref, qd_ref, knd_ref, vnd_ref, zf_ref, zd_ref, g_ref,
     lq1, lk1, lq2, lk2, xa_ref, xb_ref, mf, lf, af, md, ld, ad) = refs[4 * ppb:]
    j = pl.program_id(1)
    hp = 2 * n_heads_d
    bias_row0 = (j * ppb) % 8

    def bf(x):
        return x.astype(BF16)

    @pl.when(j == 0)
    def _():
        qn = bf(qf_ref[...]).astype(F32)
        s_new = jnp.sum(qn * bf(knf_ref[...]).astype(F32), axis=1, keepdims=True) * scale
        mf[...] = s_new
        lf[...] = jnp.ones_like(lf)
        af[...] = bf(vnf_ref[...]).astype(F32)
        for c in range(2):
            qn = bf(qd_ref[c]).astype(F32)
            s_new = jnp.sum(qn * bf(knd_ref[c]).astype(F32), axis=1, keepdims=True) * scale
            md[c * hp:(c + 1) * hp, :] = s_new
            ad[c * hp:(c + 1) * hp, :] = bf(vnd_ref[...]).astype(F32)
        ld[...] = jnp.ones_like(ld)

    qf = bf(qf_ref[...])
    s = jnp.concatenate(
        [lax.dot_general(qf, bf(r[...]), _NT, preferred_element_type=F32) * scale
         + bias_ref[pl.ds(bias_row0 + p, 1), :] for p, r in enumerate(fk_refs)], axis=1)
    pw = fk_refs[0].shape[0]
    row = lax.broadcasted_iota(jnp.int32, s.shape, 0)
    col = lax.broadcasted_iota(jnp.int32, s.shape, 1)
    s = jnp.where((col % n_heads_f) == row, s, -jnp.inf)
    m_old = mf[...]
    m_new = jnp.maximum(m_old, jnp.max(s, axis=1, keepdims=True))
    alpha = jnp.exp(m_old - m_new)
    p = jnp.exp(s - m_new)
    mf[...] = m_new
    lf[...] = alpha * lf[...] + jnp.sum(p, axis=1, keepdims=True)
    pb = bf(p)
    af[...] = alpha * af[...] + sum(
        jnp.dot(pb[:, i * pw:(i + 1) * pw], bf(r[...]), preferred_element_type=F32)
        for i, r in enumerate(fv_refs))

    pw = dk_refs[0].shape[0] // 2
    qd = [bf(qd_ref[c]) for c in range(2)]
    s = jnp.concatenate(
        [jnp.concatenate(
            [lax.dot_general(qd[c], bf(r[pl.ds(c, pw, stride=2), :]), _NT,
                             preferred_element_type=F32) for c in range(2)], axis=0)
         for r in dk_refs], axis=1) * scale
    row = lax.broadcasted_iota(jnp.int32, s.shape, 0)
    col = lax.broadcasted_iota(jnp.int32, s.shape, 1)
    s = jnp.where((col % n_heads_d) == (row % hp), s, -jnp.inf)
    m_old = md[...]
    m_new = jnp.maximum(m_old, jnp.max(s, axis=1, keepdims=True))
    alpha = jnp.exp(m_old - m_new)
    p = jnp.exp(s - m_new)
    md[...] = m_new
    ld[...] = alpha * ld[...] + jnp.sum(p, axis=1, keepdims=True)
    pb = bf(p)
    ad[...] = alpha * ad[...] + sum(
        jnp.dot(pb[:, i * pw:(i + 1) * pw], bf(r[...]), preferred_element_type=F32)
        for i, r in enumerate(dv_refs))

    @pl.when(j == pl.num_programs(1) - 1)
    def _():
        z = zf_ref[...]
        xa_ref[...] = ((af[...] / lf[...]) * (z * jax.nn.sigmoid(z))).astype(xa_ref.dtype)
        lam = _diff_lambda(lq1[...], lk1[...], lq2[...], lk2[...], lam_init)
        o = ad[0:hp, :] / ld[0:hp, :] - lam * (ad[hp:2 * hp, :] / ld[hp:2 * hp, :])
        o = o * lax.rsqrt(jnp.mean(o * o, axis=1, keepdims=True) + EPS) * g_ref[...]
        o = o * (1.0 - lam_init)
        z = zd_ref[...]
        xb_ref[...] = (o * (z * jax.nn.sigmoid(z))).astype(xb_ref.dtype)


def _decode_attention(page_table, fk, fv, dk, dv, bias, qf, knf, vnf, qd, knd, vnd, zf, zd,
                      g, lams, *, layer, n_pool, lam_init, ppb):
    db, n_pages = page_table.shape
    n_heads_f = qf.shape[1]
    hp = qd.shape[2]
    n_heads_d = hp // 2
    dvw = 2 * HEAD_DIM
    base = layer * n_pool
    assert 8 % ppb == 0 and n_pages % ppb == 0

    def pages(a):
        return [pl.BlockSpec((None,) + a.shape[1:],
                             lambda b, j, pt, p=p: (base + pt[b, j * ppb + p], 0, 0))
                for p in range(ppb)]

    def per_b(shape):
        nd = len(shape)
        return pl.BlockSpec((None,) + shape, lambda b, j, pt: (b,) + (0,) * nd)

    vec = pl.BlockSpec((1, HEAD_DIM), lambda b, j, pt: (0, 0))
    return pl.pallas_call(
        functools.partial(_decode_kernel, scale=HEAD_DIM ** -0.5, lam_init=lam_init,
                          n_heads_f=n_heads_f, n_heads_d=n_heads_d, ppb=ppb),
        grid_spec=pltpu.PrefetchScalarGridSpec(
            num_scalar_prefetch=1,
            grid=(db, n_pages // ppb),
            in_specs=pages(fk) + pages(fv) + pages(dk) + pages(dv) + [
                      pl.BlockSpec((None, 8, bias.shape[2]), lambda b, j, pt: (b, (j * ppb) // 8, 0)),
                      per_b(qf.shape[1:]), per_b(knf.shape[1:]), per_b(vnf.shape[1:]),
                      per_b(qd.shape[1:]), per_b(knd.shape[1:]), per_b(vnd.shape[1:]),
                      per_b(zf.shape[1:]), per_b(zd.shape[1:]),
                      pl.BlockSpec((1, dvw), lambda b, j, pt: (0, 0)),
                      vec, vec, vec, vec],
            out_specs=[per_b((n_heads_f, HEAD_DIM)), per_b((hp, dvw))],
            scratch_shapes=[pltpu.VMEM((n_heads_f, 1), F32), pltpu.VMEM((n_heads_f, 1), F32),
                            pltpu.VMEM((n_heads_f, HEAD_DIM), F32),
                            pltpu.VMEM((2 * hp, 1), F32), pltpu.VMEM((2 * hp, 1), F32),
                            pltpu.VMEM((2 * hp, dvw), F32)]),
        out_shape=[jax.ShapeDtypeStruct((db, n_heads_f, HEAD_DIM), BF16),
                   jax.ShapeDtypeStruct((db, hp, dvw), BF16)],
        compiler_params=pltpu.CompilerParams(
            dimension_semantics=("arbitrary", "arbitrary"), vmem_limit_bytes=V7X_VMEM_LIMIT),
        name="decode_attention",
    )(page_table, *([fk] * ppb + [fv] * ppb + [dk] * ppb + [dv] * ppb),
      bias, qf, knf, vnf, qd, knd, vnd, zf, zd, g, *lams)


def _rope_tables(pos):
    half = HEAD_DIM // 2
    inv = 1.0 / (ROPE_THETA ** (jnp.arange(half, dtype=F32) * 2.0 / HEAD_DIM))
    ang = pos.astype(F32)[:, None] * inv[None, :]
    cos, sin = jnp.cos(ang), jnp.sin(ang)
    return jnp.concatenate([cos, cos], axis=1), jnp.concatenate([-sin, sin], axis=1)


def _in_project(x, w_in_t, layer, b_f, cos2, sin2, *, bm, fw, dw, d, stack_depth=0, stacked=None):
    n_heads_f = b_f.shape[0]
    mm = functools.partial(_matmul, x, w_in_t, layer, bm=bm, bn=1024, w_rows=True)
    rope = dict(epilogue="rope", extra=(cos2, sin2))
    tail0 = 4 * fw + n_heads_f

    def kv(name, **kw):
        prev = None if stacked is None else stacked[name]
        return mm([F32, BF16], n=kw.pop("n"), name="proj_" + name, stack_depth=stack_depth,
                  stacked=prev, **kw)

    qa, = mm([BF16], w0=0, n=fw, name="proj_qa")
    ka, ka16 = kv("ka", w0=fw, n=fw)
    va, va16 = kv("va", w0=2 * fw, n=fw)
    za, = mm([F32], w0=3 * fw, n=fw, name="proj_za")
    b_pad = jnp.pad(b_f.astype(F32), (0, LANES - n_heads_f)).reshape(1, LANES)
    logf_pad, = _matmul(x, w_in_t, layer, [F32], bm=bm, bn=LANES, n=LANES, w0=4 * fw, w_rows=True,
                        epilogue="logsig", extra=(b_pad,), name="proj_fa")
    qb, = mm([BF16], w0=tail0, n=dw, name="proj_qb", **rope)
    kb, kb16 = kv("kb", w0=tail0 + dw, n=dw, **rope)
    vb, vb16 = kv("vb", w0=tail0 + 2 * dw, n=dw)
    zgb, = mm([F32], w0=tail0 + 3 * dw, n=dw + 2 * d, name="proj_gates")
    return dict(qa=qa, ka=ka, ka16=ka16, va=va, va16=va16, za=za, logf_pad=logf_pad, qb=qb, kb=kb,
                kb16=kb16, vb=vb, vb16=vb16, zgb=zgb)


def kernel(x_prompt, x_sample, cache_fox_k, cache_fox_v, cache_fox_logf, cache_diff_k, cache_diff_v,
           page_table, meta_tokens, w_in, b_forget, lambda_q1, lambda_k1, lambda_q2, lambda_k2,
           diff_norm_g, w_out_fox, w_out_diff, w_out, ln_g, ln_b):
    nb, seq, d = x_prompt.shape
    depth = w_in.shape[0]
    t = seq + N_META
    n_heads_f = b_forget.shape[1]
    fw = n_heads_f * HEAD_DIM
    dw = w_out_diff.shape[1]
    n_heads_d = dw // (2 * HEAD_DIM)
    db, n_new, _ = x_sample.shape
    n_pool, page = cache_fox_k.shape[1], cache_fox_k.shape[2]
    n_pages = page_table.shape[1]
    past = n_pages * page
    assert n_new == 1 and db <= ROW_PAD and n_pages % 8 == 0
    alpha = (2 * depth) ** 0.25
    blk = 256
    bm = t // 3
    bm_ln = max(r for r in range(ROW_PAD, 193, ROW_PAD) if (nb * t) % r == 0)
    assert t % 3 == 0 and bm % ROW_PAD == 0 and (t - N_META) % blk == 0

    cos_p, sin_p = _rope_tables(jnp.arange(t, dtype=jnp.int32))
    cos_s, sin_s = _rope_tables(jnp.full((ROW_PAD,), past, dtype=jnp.int32))

    meta = jnp.broadcast_to(meta_tokens.astype(x_prompt.dtype)[None], (nb, N_META, d))
    h = jnp.concatenate([meta, x_prompt], axis=1).reshape(nb * t, d)
    h16 = h.astype(BF16)
    hs = jnp.pad(x_sample.reshape(db, d), ((0, ROW_PAD - db), (0, 0)))
    hs16 = hs.astype(BF16)

    fk_pool = cache_fox_k.reshape(depth * n_pool, page * n_heads_f, HEAD_DIM)
    fv_pool = cache_fox_v.reshape(depth * n_pool, page * n_heads_f, HEAD_DIM)
    dk_pool = cache_diff_k.reshape(depth * n_pool, page * n_heads_d * 2, HEAD_DIM)
    dv_pool = cache_diff_v.reshape(depth * n_pool, page * n_heads_d, 2 * HEAD_DIM)
    lf_pool = cache_fox_logf.astype(F32).reshape(depth, n_pool, page * n_heads_f)

    w_in_t = jnp.swapaxes(w_in, 1, 2).astype(BF16)
    w_oa16, w_ob16, w_o16 = w_out_fox.astype(BF16), w_out_diff.astype(BF16), w_out.astype(BF16)

    leaves_p, lf_p = None, []
    outs_s = {k: [] for k in ("ka", "va", "lf", "kb", "vb")}
    row = lambda a, l: a[l].astype(F32).reshape(1, -1)

    for l in range(depth):
        lam_init = _lambda_init(l)
        lams = (row(lambda_q1, l), row(lambda_k1, l), row(lambda_q2, l), row(lambda_k2, l))
        g = row(diff_norm_g, l)
        lng, lnb = row(ln_g, l), row(ln_b, l)
        proj = functools.partial(_in_project, w_in_t=w_in_t, layer=l, b_f=b_forget[l], fw=fw, dw=dw, d=d)
        merge = functools.partial(_gated_merge, w_oa=w_oa16, w_ob=w_ob16, layer=l,
                                  ga_col0=dw, gb_col0=dw + d, bn=1024)

        p = proj(h16, cos2=cos_p, sin2=sin_p, bm=bm, stack_depth=depth, stacked=leaves_p)
        leaves_p = {k: p[k] for k in ("ka", "va", "kb", "vb")}
        r3 = lambda a: a.reshape(nb, t, -1)
        c, cts, ctm = _prompt_cumsum(r3(p["logf_pad"]), n_heads_f, blk)
        xa = _fox_attention(r3(p["qa"]), r3(p["ka16"]), r3(p["va16"]), r3(p["za"]), c, cts, ctm,
                            n_heads=n_heads_f, blk=blk)
        xb = _diff_attention(r3(p["qb"]), r3(p["kb16"]), r3(p["vb16"]), r3(p["zgb"]), 0, g, lams,
                             n_heads=n_heads_d, blk=blk, lam_init=lam_init)
        m = merge(xa.reshape(nb * t, fw), xb.reshape(nb * t, dw), zg=p["zgb"], bm=bm)
        h, h16 = _out_ln(m, w_o16, l, h, lng, lnb, bm=bm, bm_ln=bm_ln, alpha=alpha)
        lf_p.append(p["logf_pad"][:, :n_heads_f].reshape(nb, t, n_heads_f))

        s = proj(hs16, cos2=cos_s, sin2=sin_s, bm=ROW_PAD)
        lf_new = jnp.tile(s["logf_pad"][:, :n_heads_f], (1, LANES // n_heads_f))
        bias = _decode_bias(page_table, lf_pool, l, lf_new, n_heads=n_heads_f)
        hp = 2 * n_heads_d

        def maps(a):
            a = a.reshape(ROW_PAD, n_heads_d, 2, HEAD_DIM).transpose(0, 2, 1, 3)
            return jnp.pad(a, ((0, 0), (0, 0), (0, hp - n_heads_d), (0, 0)))

        def vrows(a):
            a = a.reshape(ROW_PAD, n_heads_d, 2 * HEAD_DIM)
            return jnp.pad(a, ((0, 0), (0, hp - n_heads_d), (0, 0)))

        xa_s, xb_s = _decode_attention(
            page_table, fk_pool, fv_pool, dk_pool, dv_pool, bias,
            s["qa"].astype(F32).reshape(ROW_PAD, n_heads_f, HEAD_DIM),
            s["ka"].reshape(ROW_PAD, n_heads_f, HEAD_DIM),
            s["va"].reshape(ROW_PAD, n_heads_f, HEAD_DIM),
            maps(s["qb"].astype(F32)), maps(s["kb"]), vrows(s["vb"]),
            s["za"].reshape(ROW_PAD, n_heads_f, HEAD_DIM), vrows(s["zgb"][:, :dw]),
            g, lams, layer=l, n_pool=n_pool, lam_init=lam_init, ppb=4)
        xa_s = jnp.pad(xa_s.reshape(db, fw), ((0, ROW_PAD - db), (0, 0)))
        xb_s = jnp.pad(xb_s[:, :n_heads_d].reshape(db, dw), ((0, ROW_PAD - db), (0, 0)))
        ms = merge(xa_s, xb_s, zg=s["zgb"], bm=ROW_PAD)
        hs, hs16 = _out_ln(ms, w_o16, l, hs, lng, lnb, bm=ROW_PAD, bm_ln=ROW_PAD, alpha=alpha)
        outs_s["ka"].append(s["ka"][:db].reshape(db, 1, n_heads_f, HEAD_DIM))
        outs_s["va"].append(s["va"][:db].reshape(db, 1, n_heads_f, HEAD_DIM))
        outs_s["lf"].append(s["logf_pad"][:db, :n_heads_f].reshape(db, 1, n_heads_f))
        outs_s["kb"].append(s["kb"][:db].reshape(db, 1, n_heads_d, 2, HEAD_DIM))
        outs_s["vb"].append(s["vb"][:db].reshape(db, 1, n_heads_d, 2 * HEAD_DIM))

    y_prompt = h.reshape(nb, t, d)[:, N_META:]
    y_sample = hs[:db].reshape(db, 1, d)
    st = jnp.stack
    return (y_prompt, y_sample,
            leaves_p["ka"].reshape(depth, nb, t, n_heads_f, HEAD_DIM),
            leaves_p["va"].reshape(depth, nb, t, n_heads_f, HEAD_DIM),
            st(lf_p),
            leaves_p["kb"].reshape(depth, nb, t, n_heads_d, 2, HEAD_DIM),
            leaves_p["vb"].reshape(depth, nb, t, n_heads_d, 2 * HEAD_DIM),
            st(outs_s["ka"]), st(outs_s["va"]), st(outs_s["lf"]), st(outs_s["kb"]), st(outs_s["vb"]))
```
